```python
import math
import jax
import jax.numpy as jnp
from jax import lax
import numpy as np

D_MODEL = 1024
BATCH = 4
SEQ = 4096
DEPTH = 1
DEC_BATCH = 128
DEC_SEQ = 8
PAST_LEN = 8192
PAGE_SIZE = 128

N_MEM = 256
DA_HEADS = 4
DA_HEAD_DIM = D_MODEL // 16
DA_V_DIM = 2 * DA_HEAD_DIM
RET_HEADS = 4
RET_QK_DIM = D_MODEL // 8
RET_V_DIM = D_MODEL // 8
CA_HEADS = 4
CA_HEAD_DIM = D_MODEL // CA_HEADS
D_FF = 4 * D_MODEL
DA_QK_W = DA_HEADS * 2 * DA_HEAD_DIM
DA_V_W = DA_HEADS * DA_V_DIM
RET_QK_W = RET_HEADS * RET_QK_DIM
RET_V_W = RET_HEADS * RET_V_DIM
IN_W = 2 * DA_QK_W + DA_V_W + 2 * RET_QK_W + 2 * RET_V_W
MIX_W = DA_V_W + RET_V_W
ROPE_THETA = 10000.0
RMS_EPS = 1e-6
Q_BLOCK = 128
RET_CHUNK = 128

kernel_name = 'hymba_style_diffattn_retention_decoder_step'


def _rmsnorm(x, g):
    xf = x.astype(jnp.float32)
    y = xf * lax.rsqrt(jnp.mean(xf * xf, axis=-1, keepdims=True) + RMS_EPS)
    return (y * g.astype(jnp.float32)).astype(x.dtype)


def _rope_tables(pos, dim):
    inv = 1.0 / (ROPE_THETA ** (jnp.arange(0, dim, 2, dtype=jnp.float32) / dim))
    ang = pos.astype(jnp.float32)[:, None] * inv[None, :]
    return jnp.cos(ang), jnp.sin(ang)


def _rope(x, cos, sin):
    c = cos[None, :, None, :].astype(x.dtype)
    s = sin[None, :, None, :].astype(x.dtype)
    x1, x2 = jnp.split(x, 2, axis=-1)
    return jnp.concatenate([x1 * c - x2 * s, x1 * s + x2 * c], axis=-1)


def _project_mixer(h, w_in, pos):
    B, T, _ = h.shape
    splits = [DA_QK_W, 2 * DA_QK_W, 2 * DA_QK_W + DA_V_W,
              2 * DA_QK_W + DA_V_W + RET_QK_W, 2 * DA_QK_W + DA_V_W + 2 * RET_QK_W,
              2 * DA_QK_W + DA_V_W + 2 * RET_QK_W + RET_V_W]
    dq, dk, dv, rq, rk, rv, rg = jnp.split(h @ w_in, splits, axis=-1)
    cd, sd = _rope_tables(pos, DA_HEAD_DIM)
    dq = _rope(dq.reshape(B, T, DA_HEADS * 2, DA_HEAD_DIM), cd, sd).reshape(B, T, DA_HEADS, 2, DA_HEAD_DIM)
    dk = _rope(dk.reshape(B, T, DA_HEADS * 2, DA_HEAD_DIM), cd, sd).reshape(B, T, DA_HEADS, 2, DA_HEAD_DIM)
    dv = dv.reshape(B, T, DA_HEADS, DA_V_DIM)
    cr, sr = _rope_tables(pos, RET_QK_DIM)
    rq = _rope(rq.reshape(B, T, RET_HEADS, RET_QK_DIM), cr, sr)
    rk = _rope(rk.reshape(B, T, RET_HEADS, RET_QK_DIM), cr, sr) * (RET_QK_DIM ** -0.5)
    rv = rv.reshape(B, T, RET_HEADS, RET_V_DIM)
    return dq, dk, dv, rq, rk, rv, rg


def _diff_lambda(lq1, lk1, lq2, lk2, lam_init):
    f = jnp.float32
    return (jnp.exp(jnp.sum(lq1.astype(f) * lk1.astype(f)))
            - jnp.exp(jnp.sum(lq2.astype(f) * lk2.astype(f))) + lam_init)


def _diff_weights(s, lam):
    a = jax.nn.softmax(s, axis=-1)
    return a[:, :, 0] - lam * a[:, :, 1]


def _diff_attn_prompt(q, k, v, lam):
    B, T, H, _, d = q.shape
    q = q * (d ** -0.5)
    kpos = jnp.arange(T)

    def one_block(i):
        qb = lax.dynamic_slice_in_dim(q, i * Q_BLOCK, Q_BLOCK, axis=1)
        s = jnp.einsum('bqhcd,bkhcd->bhcqk', qb, k).astype(jnp.float32)
        qpos = i * Q_BLOCK + jnp.arange(Q_BLOCK)
        s = jnp.where(kpos[None, :] <= qpos[:, None], s, -jnp.inf)
        w = _diff_weights(s, lam).astype(v.dtype)
        return jnp.einsum('bhqk,bkhv->bqhv', w, v)

    o = lax.map(one_block, jnp.arange(T // Q_BLOCK))
    return jnp.moveaxis(o, 0, 1).reshape(B, T, H, v.shape[-1])


def _diff_attn_sample(q, k_new, v_new, k_past, v_past, lam):
    S = q.shape[1]
    d = q.shape[-1]
    P = k_past.shape[1]
    q = q * (d ** -0.5)
    s_past = jnp.einsum('bqhcd,bkhcd->bhcqk', q, k_past)
    s_new = jnp.einsum('bqhcd,bkhcd->bhcqk', q, k_new)
    s_new = jnp.where(jnp.tril(jnp.ones((S, S), dtype=bool)), s_new, -jnp.inf)
    s = jnp.concatenate([s_past, s_new], axis=-1).astype(jnp.float32)
    w = _diff_weights(s, lam).astype(v_new.dtype)
    return (jnp.einsum('bhqk,bkhv->bqhv', w[..., :P], v_past)
            + jnp.einsum('bhqk,bkhv->bqhv', w[..., P:], v_new))


def _retention_chunk(S, q, k, v, log_gamma):
    f = jnp.float32
    S = S.astype(f)
    q, k, v = q.astype(f), k.astype(f), v.astype(f)
    C = q.shape[1]
    i = jnp.arange(C, dtype=f)
    diff = i[:, None] - i[None, :]
    decay = jnp.where(diff >= 0, jnp.exp(jnp.maximum(diff, 0.0)[None] * log_gamma[:, None, None]), 0.0)
    inner = jnp.einsum('bihd,bjhd->bhij', q, k) * decay[None]
    o = jnp.einsum('bhij,bjhv->bihv', inner, v)
    q_decay = jnp.exp((i + 1.0)[:, None] * log_gamma[None, :])
    o = o + jnp.einsum('bihd,bhdv->bihv', q, S) * q_decay[None, :, :, None]
    k_decay = jnp.exp((C - 1.0 - i)[:, None] * log_gamma[None, :])
    S_new = (jnp.exp(C * log_gamma)[None, :, None, None] * S
             + jnp.einsum('bjhd,bjhv->bhdv', k * k_decay[None, :, :, None], v))
    return S_new, o


def _retention_prompt(q, k, v, log_gamma):
    B, T, H, dk = q.shape
    dv = v.shape[-1]
    nc = T // RET_CHUNK

    def to_chunks(a):
        return jnp.moveaxis(a.reshape(B, nc, RET_CHUNK, H, a.shape[-1]), 1, 0)

    S0 = jnp.zeros((B, H, dk, dv), jnp.float32)
    S, o = lax.scan(lambda s, xs: _retention_chunk(s, xs[0], xs[1], xs[2], log_gamma),
                    S0, (to_chunks(q), to_chunks(k), to_chunks(v)))
    return S, jnp.moveaxis(o, 0, 1).reshape(B, T, H, dv)


def _mix_merge(da, ro, rg, g_sub, g_ret, w_out, lam_init):
    B, T = rg.shape[:2]
    da = (_rmsnorm(da, g_sub) * (1.0 - lam_init)).reshape(B, T, DA_V_W)
    ro = _rmsnorm(ro.astype(rg.dtype), g_ret).reshape(B, T, RET_V_W) * jax.nn.silu(rg)
    return jnp.concatenate([da, ro], axis=-1) @ w_out


def _mem_kv(mem, g_mem, w_ck, w_cv):
    B = mem.shape[0]
    m = _rmsnorm(mem, g_mem)
    return ((m @ w_ck).reshape(B, N_MEM, CA_HEADS, CA_HEAD_DIM),
            (m @ w_cv).reshape(B, N_MEM, CA_HEADS, CA_HEAD_DIM))


def _cross_attn(h, mem_k, mem_v, w_cq, w_co):
    B, T, _ = h.shape
    q = (h @ w_cq).reshape(B, T, CA_HEADS, CA_HEAD_DIM) * (CA_HEAD_DIM ** -0.5)
    s = jnp.einsum('bqhd,bkhd->bhqk', q, mem_k.astype(q.dtype)).astype(jnp.float32)
    a = jax.nn.softmax(s, axis=-1).astype(h.dtype)
    o = jnp.einsum('bhqk,bkhd->bqhd', a, mem_v.astype(h.dtype)).reshape(B, T, CA_HEADS * CA_HEAD_DIM)
    return o @ w_co


def _mlp(h, w_up, w_down):
    return jnp.square(jax.nn.relu(h @ w_up)) @ w_down


def setup_inputs(seed: int = 0) -> dict:
    key = jax.random.key(seed)
    ks = jax.random.split(key, 32)
    f = jnp.float32
    n_pages = PAST_LEN // PAGE_SIZE
    n_pool = (DEC_BATCH * n_pages * 5) // 4
    nrm = lambda k, shape, scale: jax.random.normal(k, shape, f) * scale
    gain = lambda k, shape: 1.0 + 0.02 * jax.random.normal(k, shape, f)
    page_table = jax.random.permutation(ks[7], n_pool)[:DEC_BATCH * n_pages].reshape(DEC_BATCH, n_pages).astype(jnp.int32)
    return {
        'x_prompt': nrm(ks[0], (BATCH, SEQ, D_MODEL), 1.0),
        'x_sample': nrm(ks[1], (DEC_BATCH, DEC_SEQ, D_MODEL), 1.0),
        'cache_k': nrm(ks[2], (DEPTH, n_pool, PAGE_SIZE, DA_HEADS, 2 * DA_HEAD_DIM), 1.0),
        'cache_v': nrm(ks[3], (DEPTH, n_pool, PAGE_SIZE, DA_HEADS, DA_V_DIM), 1.0),
        'state_ret': nrm(ks[4], (DEPTH, DEC_BATCH, RET_HEADS, RET_QK_DIM, RET_V_DIM), 0.1),
        'cache_mem_k': nrm(ks[5], (DEPTH, DEC_BATCH, N_MEM, CA_HEADS, CA_HEAD_DIM), 1.0),
        'cache_mem_v': nrm(ks[6], (DEPTH, DEC_BATCH, N_MEM, CA_HEADS, CA_HEAD_DIM), 1.0),
        'page_table': page_table,
        'mem_prompt': nrm(ks[8], (BATCH, N_MEM, D_MODEL), 1.0),
        'g_mix': gain(ks[9], (DEPTH, D_MODEL)),
        'w_in': nrm(ks[10], (DEPTH, D_MODEL, IN_W), D_MODEL ** -0.5),
        'lambda_q1': nrm(ks[11], (DEPTH, DA_HEAD_DIM), 0.1),
        'lambda_k1': nrm(ks[12], (DEPTH, DA_HEAD_DIM), 0.1),
        'lambda_q2': nrm(ks[13], (DEPTH, DA_HEAD_DIM), 0.1),
        'lambda_k2': nrm(ks[14], (DEPTH, DA_HEAD_DIM), 0.1),
        'g_diff_sub': gain(ks[15], (DEPTH, DA_V_DIM)),
        'g_ret': gain(ks[16], (DEPTH, RET_V_DIM)),
        'w_out': nrm(ks[17], (DEPTH, MIX_W, D_MODEL), MIX_W ** -0.5),
        'g_cross': gain(ks[18], (DEPTH, D_MODEL)),
        'g_mem': gain(ks[19], (DEPTH, D_MODEL)),
        'w_cq': nrm(ks[20], (DEPTH, D_MODEL, CA_HEADS * CA_HEAD_DIM), D_MODEL ** -0.5),
        'w_ck': nrm(ks[21], (DEPTH, D_MODEL, CA_HEADS * CA_HEAD_DIM), D_MODEL ** -0.5),
        'w_cv': nrm(ks[22], (DEPTH, D_MODEL, CA_HEADS * CA_HEAD_DIM), D_MODEL ** -0.5),
        'w_co': nrm(ks[23], (DEPTH, CA_HEADS * CA_HEAD_DIM, D_MODEL), (CA_HEADS * CA_HEAD_DIM) ** -0.5),
        'g_mlp': gain(ks[24], (DEPTH, D_MODEL)),
        'w_up': nrm(ks[25], (DEPTH, D_MODEL, D_FF), D_MODEL ** -0.5),
        'w_down': nrm(ks[26], (DEPTH, D_FF, D_MODEL), D_FF ** -0.5),
        'g_final': gain(ks[27], (D_MODEL,)),
    }


def reference(x_prompt, x_sample, cache_k, cache_v, state_ret, cache_mem_k, cache_mem_v, page_table,
              mem_prompt, g_mix, w_in, lambda_q1, lambda_k1, lambda_q2, lambda_k2, g_diff_sub, g_ret,
              w_out, g_cross, g_mem, w_cq, w_ck, w_cv, w_co, g_mlp, w_up, w_down, g_final):
    Bp, T, _ = x_prompt.shape
    Bd, S, _ = x_sample.shape
    past = page_table.shape[1] * cache_k.shape[2]
    pos_p = jnp.arange(T)
    pos_s = past + jnp.arange(S)
    log_gamma = jnp.log(1.0 - 2.0 ** (-5.0 - jnp.arange(RET_HEADS, dtype=jnp.float32)))

    yp, ys = x_prompt, x_sample
    kp_l, vp_l, sp_l, mkp_l, mvp_l, ks_l, vs_l, ss_l = [], [], [], [], [], [], [], []
    for l in range(DEPTH):
        lam_init = 0.8 - 0.6 * math.exp(-0.3 * l)
        lam = _diff_lambda(lambda_q1[l], lambda_k1[l], lambda_q2[l], lambda_k2[l], lam_init)

        h = _rmsnorm(yp, g_mix[l])
        dq, dk, dv, rq, rk, rv, rg = _project_mixer(h, w_in[l], pos_p)
        da = _diff_attn_prompt(dq, dk, dv, lam)
        S_p, ro = _retention_prompt(rq, rk, rv, log_gamma)
        yp = yp + _mix_merge(da, ro, rg, g_diff_sub[l], g_ret[l], w_out[l], lam_init)
        mk, mv = _mem_kv(mem_prompt, g_mem[l], w_ck[l], w_cv[l])
        yp = yp + _cross_attn(_rmsnorm(yp, g_cross[l]), mk, mv, w_cq[l], w_co[l])
        yp = yp + _mlp(_rmsnorm(yp, g_mlp[l]), w_up[l], w_down[l])
        kp_l.append(dk.reshape(Bp, T, DA_HEADS, 2 * DA_HEAD_DIM))
        vp_l.append(dv)
        sp_l.append(S_p)
        mkp_l.append(mk)
        mvp_l.append(mv)

        h = _rmsnorm(ys, g_mix[l])
        dq, dk, dv, rq, rk, rv, rg = _project_mixer(h, w_in[l], pos_s)
        k_past = cache_k[l, page_table].reshape(Bd, past, DA_HEADS, 2, DA_HEAD_DIM).astype(dk.dtype)
        v_past = cache_v[l, page_table].reshape(Bd, past, DA_HEADS, DA_V_DIM).astype(dv.dtype)
        da = _diff_attn_sample(dq, dk, dv, k_past, v_past, lam)
        S_s, ro = _retention_chunk(state_ret[l], rq, rk, rv, log_gamma)
        ys = ys + _mix_merge(da, ro, rg, g_diff_sub[l], g_ret[l], w_out[l], lam_init)
        ys = ys + _cross_attn(_rmsnorm(ys, g_cross[l]), cache_mem_k[l], cache_mem_v[l], w_cq[l], w_co[l])
        ys = ys + _mlp(_rmsnorm(ys, g_mlp[l]), w_up[l], w_down[l])
        ks_l.append(dk.reshape(Bd, S, DA_HEADS, 2 * DA_HEAD_DIM))
        vs_l.append(dv)
        ss_l.append(S_s)

    y_prompt = _rmsnorm(yp, g_final)
    y_sample = _rmsnorm(ys, g_final)
    return (y_prompt, y_sample, jnp.stack(kp_l), jnp.stack(vp_l), jnp.stack(sp_l), jnp.stack(mkp_l),
            jnp.stack(mvp_l), jnp.stack(ks_l), jnp.stack(vs_l), jnp.stack(ss_l))
```

```python
import functools
import math

import jax
import jax.numpy as jnp
from jax import lax
from jax.experimental import pallas as pl
from jax.experimental.pallas import tpu as pltpu

F32 = jnp.float32
BF16 = jnp.bfloat16

D_MODEL = 1024
DA_HEADS = 4
DA_HEAD_DIM = 64
DA_V_DIM = 128
RET_HEADS = 4
RET_QK_DIM = 128
RET_V_DIM = 128
CA_HEADS = 4
CA_HEAD_DIM = 256
D_FF = 4096
GROUP_W = 512
N_GROUPS = 7
ROPE_THETA = 10000.0
RMS_EPS = 1e-6
RET_CHUNK = 128
LANES = 128
VMEM_LIMIT = 56 * 1024 * 1024

LOG_GAMMA = tuple(math.log(1.0 - 2.0 ** (-5.0 - h)) for h in range(RET_HEADS))


def _dot(a, b):
    return jnp.dot(a, b, preferred_element_type=F32)


def _dot_nt(a, b):
    return lax.dot_general(a, b, (((1,), (1,)), ((), ())), preferred_element_type=F32)


def _dot_tn(a, b):
    return lax.dot_general(a, b, (((0,), (0,)), ((), ())), preferred_element_type=F32)


def _rms(x, g):
    ms = jnp.mean(x * x, axis=-1, keepdims=True)
    return x * lax.rsqrt(ms + RMS_EPS) * g


def _diff_lambda(lq1, lk1, lq2, lk2, lam_init):
    a = jnp.sum(lq1[...] * lk1[...], axis=-1, keepdims=True)
    b = jnp.sum(lq2[...] * lk2[...], axis=-1, keepdims=True)
    return jnp.exp(a) - jnp.exp(b) + lam_init


def _params(sem):
    return pltpu.CompilerParams(dimension_semantics=sem, vmem_limit_bytes=VMEM_LIMIT)


def _const_spec(shape):
    nd = len(shape)
    return pl.BlockSpec(shape, lambda *_: (0,) * nd)


def _in_proj_kernel(x_ref, g_ref, w_ref, cd_ref, sd_ref, cr_ref, sr_ref,
                    dq_ref, dk_ref, dv_ref, rq_ref, rk_ref, rv_ref, rg_ref):
    h = _rms(x_ref[...], g_ref[...]).astype(BF16)
    bm = h.shape[0]
    lane = lax.broadcasted_iota(jnp.int32, (bm, LANES), 1)
    first_half = (lane & (DA_HEAD_DIM - 1)) < (DA_HEAD_DIM // 2)
    cd, sd, cr, sr = cd_ref[...], sd_ref[...], cr_ref[...], sr_ref[...]

    def proj(g):
        return _dot(h, w_ref[:, g * GROUP_W:(g + 1) * GROUP_W])

    def rope_da(x):
        sw = jnp.where(first_half, pltpu.roll(x, LANES - 32, 1), pltpu.roll(x, 32, 1))
        return x * cd + sw * sd

    def rope_ret(x):
        return x * cr + pltpu.roll(x, 64, 1) * sr

    def store(ref, x, fn):
        for c in range(GROUP_W // LANES):
            sl = slice(c * LANES, (c + 1) * LANES)
            ref[:, sl] = fn(x[:, sl])

    store(dq_ref, proj(0), lambda x: rope_da(x) * (DA_HEAD_DIM ** -0.5))
    store(dk_ref, proj(1), rope_da)
    dv_ref[...] = proj(2)
    store(rq_ref, proj(3), rope_ret)
    store(rk_ref, proj(4), lambda x: rope_ret(x) * (RET_QK_DIM ** -0.5))
    rv_ref[...] = proj(5)
    rg_ref[...] = proj(6)


def _in_proj(x, g, w_bf16, tabs, bm, tab_blocks):
    m = x.shape[0]
    row = lambda i: (i, 0)
    tab = lambda i: (i % tab_blocks, 0)
    out_spec = pl.BlockSpec((bm, GROUP_W), row)
    return pl.pallas_call(
        _in_proj_kernel,
        grid=(m // bm,),
        in_specs=[pl.BlockSpec((bm, D_MODEL), row),
                  _const_spec((1, D_MODEL)),
                  _const_spec((D_MODEL, N_GROUPS * GROUP_W))]
                 + [pl.BlockSpec((bm, LANES), tab)] * 4,
        out_specs=[out_spec] * N_GROUPS,
        out_shape=[jax.ShapeDtypeStruct((m, GROUP_W), F32)] * N_GROUPS,
        compiler_params=_params(("parallel",)),
        name="in_proj",
    )(x, g, w_bf16, *tabs)


def _rope_tables(pos, dim):
    inv = 1.0 / (ROPE_THETA ** (jnp.arange(0, dim, 2, dtype=F32) / dim))
    ang = pos.astype(F32)[:, None] * inv[None, :]
    c, s = jnp.cos(ang), jnp.sin(ang)
    reps = LANES // dim
    return (jnp.tile(jnp.concatenate([c, c], axis=-1), (1, reps)),
            jnp.tile(jnp.concatenate([-s, s], axis=-1), (1, reps)))


def _softmax_step(s, v, m_ref, l_ref, acc_ref):
    m_old = m_ref[...]
    m_new = jnp.maximum(m_old, jnp.max(s, axis=-1, keepdims=True))
    alpha = jnp.exp(m_old - m_new)
    p = jnp.exp(s - m_new)
    l_ref[...] = alpha * l_ref[...] + jnp.sum(p, axis=-1, keepdims=True)
    acc_ref[...] = alpha * acc_ref[...] + _dot(p.astype(BF16), v)
    m_ref[...] = m_new


def _dattn_prompt_kernel(qi_ref, kj_ref, q_ref, k_ref, v_ref, lq1, lk1, lq2, lk2,
                         o_ref, qs_ref, m_ref, l_ref, acc_ref, *, lam_init):
    step = pl.program_id(2)
    qi, kj = qi_ref[step], kj_ref[step]
    bq = q_ref.shape[1]
    bk = k_ref.shape[1]

    @pl.when(kj == 0)
    def _():
        q = q_ref[0]
        lane = lax.broadcasted_iota(jnp.int32, q.shape, 1)
        q1 = jnp.where(lane < DA_HEAD_DIM, q, 0.0)
        q2 = jnp.where(lane >= DA_HEAD_DIM, q, 0.0)
        qs_ref[...] = jnp.concatenate([q1, q2], axis=0).astype(BF16)
        m_ref[...] = jnp.full(m_ref.shape, -jnp.inf, F32)
        l_ref[...] = jnp.zeros(l_ref.shape, F32)
        acc_ref[...] = jnp.zeros(acc_ref.shape, F32)

    k = k_ref[0].astype(BF16)
    v = v_ref[0].astype(BF16)

    @pl.when(kj < qi)
    def _():
        _softmax_step(_dot_nt(qs_ref[...], k), v, m_ref, l_ref, acc_ref)

    @pl.when(kj == qi)
    def _():
        s = _dot_nt(qs_ref[...], k)
        qpos = lax.broadcasted_iota(jnp.int32, s.shape, 0) & (bq - 1)
        kpos = lax.broadcasted_iota(jnp.int32, s.shape, 1)
        _softmax_step(jnp.where(kpos <= qpos, s, -jnp.inf), v, m_ref, l_ref, acc_ref)
        lam = _diff_lambda(lq1, lk1, lq2, lk2, lam_init)
        o = acc_ref[...] / l_ref[...]
        o_ref[0] = o[:bq] - lam * o[bq:]


def _dattn_prompt(dq, dk, dv, lams, lam_init, blk):
    b, t, _ = dq.shape
    nq = t // blk
    assert blk & (blk - 1) == 0
    qi = jnp.asarray([i for i in range(nq) for _ in range(i + 1)], jnp.int32)
    kj = jnp.asarray([j for i in range(nq) for j in range(i + 1)], jnp.int32)
    qmap = lambda bi, h, s, qi_r, kj_r: (bi, qi_r[s], h)
    kmap = lambda bi, h, s, qi_r, kj_r: (bi, kj_r[s], h)
    lspec = pl.BlockSpec((1, DA_HEAD_DIM), lambda *_: (0, 0))
    return pl.pallas_call(
        functools.partial(_dattn_prompt_kernel, lam_init=lam_init),
        grid_spec=pltpu.PrefetchScalarGridSpec(
            num_scalar_prefetch=2,
            grid=(b, DA_HEADS, int(qi.shape[0])),
            in_specs=[pl.BlockSpec((1, blk, LANES), qmap),
                      pl.BlockSpec((1, blk, LANES), kmap),
                      pl.BlockSpec((1, blk, LANES), kmap)] + [lspec] * 4,
            out_specs=pl.BlockSpec((1, blk, LANES), qmap),
            scratch_shapes=[pltpu.VMEM((2 * blk, LANES), BF16),
                            pltpu.VMEM((2 * blk, 1), F32),
                            pltpu.VMEM((2 * blk, 1), F32),
                            pltpu.VMEM((2 * blk, LANES), F32)]),
        out_shape=jax.ShapeDtypeStruct((b, t, DA_HEADS * DA_V_DIM), F32),
        compiler_params=_params(("parallel", "parallel", "arbitrary")),
        name="dattn_prompt",
    )(qi, kj, dq, dk, dv, *lams)


def _ret_chunk(q, k, v, state, lg, c_len):
    cq, ck = q.shape[0], k.shape[0]
    i = lax.broadcasted_iota(jnp.int32, (cq, ck), 0)
    j = lax.broadcasted_iota(jnp.int32, (cq, ck), 1)
    diff = (i - j).astype(F32)
    decay = jnp.where(diff >= 0, jnp.exp(jnp.maximum(diff, 0.0) * lg), 0.0)
    qb, vb = q.astype(BF16), v.astype(BF16)
    inner = _dot_nt(qb, k.astype(BF16)) * decay
    o = _dot(inner.astype(BF16), vb)
    iq = lax.broadcasted_iota(jnp.int32, (cq, 1), 0).astype(F32)
    o = o + _dot(qb, state.astype(BF16)) * jnp.exp((iq + 1.0) * lg)
    ik = lax.broadcasted_iota(jnp.int32, (ck, 1), 0).astype(F32)
    k_decay = jnp.where(ik < c_len, jnp.exp((c_len - 1.0 - ik) * lg), 0.0)
    new_state = math.exp(c_len * lg) * state + _dot_tn((k * k_decay).astype(BF16), vb)
    return o, new_state


def _ret_prompt_kernel(q_ref, k_ref, v_ref, o_ref, s_ref):
    @pl.when(pl.program_id(1) == 0)
    def _():
        s_ref[...] = jnp.zeros(s_ref.shape, F32)

    for h in range(RET_HEADS):
        sl = slice(h * LANES, (h + 1) * LANES)
        o, s_new = _ret_chunk(q_ref[0, :, sl], k_ref[0, :, sl], v_ref[0, :, sl],
                              s_ref[0, h], LOG_GAMMA[h], RET_CHUNK)
        o_ref[0, :, sl] = o
        s_ref[0, h] = s_new


def _ret_prompt(rq, rk, rv):
    b, t, w = rq.shape
    spec = pl.BlockSpec((1, RET_CHUNK, w), lambda bi, c: (bi, c, 0))
    return pl.pallas_call(
        _ret_prompt_kernel,
        grid=(b, t // RET_CHUNK),
        in_specs=[spec] * 3,
        out_specs=[spec, pl.BlockSpec((1, RET_HEADS, RET_QK_DIM, RET_V_DIM),
                                      lambda bi, c: (bi, 0, 0, 0))],
        out_shape=[jax.ShapeDtypeStruct((b, t, w), F32),
                   jax.ShapeDtypeStruct((b, RET_HEADS, RET_QK_DIM, RET_V_DIM), F32)],
        compiler_params=_params(("parallel", "arbitrary")),
        name="ret_prompt",
    )(rq, rk, rv)


def _ret_sample_kernel(q_ref, k_ref, v_ref, s_ref, o_ref, sn_ref):
    c_len = q_ref.shape[1]
    pad = jnp.zeros((LANES - c_len, LANES), F32)
    for h in range(RET_HEADS):
        sl = slice(h * LANES, (h + 1) * LANES)
        k = jnp.concatenate([k_ref[0, :, sl], pad], axis=0)
        v = jnp.concatenate([v_ref[0, :, sl], pad], axis=0)
        o, s_new = _ret_chunk(q_ref[0, :, sl], k, v, s_ref[0, h], LOG_GAMMA[h], c_len)
        o_ref[0, :, sl] = o
        sn_ref[0, h] = s_new


def _ret_sample(rq, rk, rv, state):
    b, s, w = rq.shape
    spec = pl.BlockSpec((1, s, w), lambda bi: (bi, 0, 0))
    sspec = pl.BlockSpec((1, RET_HEADS, RET_QK_DIM, RET_V_DIM), lambda bi: (bi, 0, 0, 0))
    return pl.pallas_call(
        _ret_sample_kernel,
        grid=(b,),
        in_specs=[spec] * 3 + [sspec],
        out_specs=[spec, sspec],
        out_shape=[jax.ShapeDtypeStruct((b, s, w), F32),
                   jax.ShapeDtypeStruct(state.shape, F32)],
        compiler_params=_params(("parallel",)),
        name="ret_sample",
    )(rq, rk, rv, state)


def _merge_kernel(da_ref, ro_ref, rg_ref, x_ref, gs_ref, gr_ref, w_ref, y_ref, *, lam_init):
    parts = []
    for h in range(DA_HEADS):
        sl = slice(h * LANES, (h + 1) * LANES)
        parts.append(_rms(da_ref[:, sl], gs_ref[...]) * (1.0 - lam_init))
    for h in range(RET_HEADS):
        sl = slice(h * LANES, (h + 1) * LANES)
        g = rg_ref[:, sl]
        parts.append(_rms(ro_ref[:, sl], gr_ref[...]) * (g * (1.0 / (1.0 + jnp.exp(-g)))))
    mix = jnp.concatenate(parts, axis=-1).astype(BF16)
    y_ref[...] = x_ref[...] + _dot(mix, w_ref[...])


def _merge(da, ro, rg, x, g_sub, g_ret, w_bf16, lam_init, bm):
    m = x.shape[0]
    row = lambda i: (i, 0)
    return pl.pallas_call(
        functools.partial(_merge_kernel, lam_init=lam_init),
        grid=(m // bm,),
        in_specs=[pl.BlockSpec((bm, GROUP_W), row)] * 3
                 + [pl.BlockSpec((bm, D_MODEL), row),
                    _const_spec((1, LANES)), _const_spec((1, LANES)),
                    _const_spec((D_MODEL, D_MODEL))],
        out_specs=pl.BlockSpec((bm, D_MODEL), row),
        out_shape=jax.ShapeDtypeStruct((m, D_MODEL), F32),
        compiler_params=_params(("parallel",)),
        name="merge_out",
    )(da, ro, rg, x, g_sub, g_ret, w_bf16)


def _mem_kv_kernel(x_ref, g_ref, wk_ref, wv_ref, k_ref, v_ref):
    h = _rms(x_ref[...], g_ref[...]).astype(BF16)
    k_ref[...] = _dot(h, wk_ref[...])
    v_ref[...] = _dot(h, wv_ref[...])


def _mem_kv(x, g, wk_bf16, wv_bf16, bm):
    m = x.shape[0]
    row = lambda i: (i, 0)
    spec = pl.BlockSpec((bm, D_MODEL), row)
    return pl.pallas_call(
        _mem_kv_kernel,
        grid=(m // bm,),
        in_specs=[spec, _const_spec((1, D_MODEL)),
                  _const_spec((D_MODEL, D_MODEL)), _const_spec((D_MODEL, D_MODEL))],
        out_specs=[spec, spec],
        out_shape=[jax.ShapeDtypeStruct((m, D_MODEL), F32)] * 2,
        compiler_params=_params(("parallel",)),
        name="mem_kv",
    )(x, g, wk_bf16, wv_bf16)


def _cross_kernel(y_ref, mk_ref, mv_ref, g_ref, wq_ref, wo_ref, o_ref):
    y = y_ref[0]
    h = _rms(y, g_ref[...]).astype(BF16)
    q = (_dot(h, wq_ref[...]) * (CA_HEAD_DIM ** -0.5)).astype(BF16)
    outs = []
    for hd in range(CA_HEADS):
        sl = slice(hd * CA_HEAD_DIM, (hd + 1) * CA_HEAD_DIM)
        s = _dot_nt(q[:, sl], mk_ref[0, :, sl].astype(BF16))
        p = jnp.exp(s - jnp.max(s, axis=-1, keepdims=True))
        a = p / jnp.sum(p, axis=-1, keepdims=True)
        outs.append(_dot(a.astype(BF16), mv_ref[0, :, sl].astype(BF16)))
    o = jnp.concatenate(outs, axis=-1).astype(BF16)
    o_ref[0] = y + _dot(o, wo_ref[...])


def _cross(y, mk, mv, g, wq_bf16, wo_bf16, bt):
    b, t, _ = y.shape
    n_mem = mk.shape[1]
    yspec = pl.BlockSpec((1, bt, D_MODEL), lambda bi, ti: (bi, ti, 0))
    mspec = pl.BlockSpec((1, n_mem, D_MODEL), lambda bi, ti: (bi, 0, 0))
    return pl.pallas_call(
        _cross_kernel,
        grid=(b, t // bt),
        in_specs=[yspec, mspec, mspec, _const_spec((1, D_MODEL)),
                  _const_spec((D_MODEL, D_MODEL)), _const_spec((D_MODEL, D_MODEL))],
        out_specs=yspec,
        out_shape=jax.ShapeDtypeStruct(y.shape, F32),
        compiler_params=_params(("parallel", "arbitrary")),
        name="cross_attn",
    )(y, mk, mv, g, wq_bf16, wo_bf16)


def _mlp_kernel(y_ref, g_ref, wu_ref, wd_ref, gf_ref, o_ref, *, ff_chunk):
    y = y_ref[...]
    h = _rms(y, g_ref[...]).astype(BF16)
    acc = y
    for c in range(D_FF // ff_chunk):
        sl = slice(c * ff_chunk, (c + 1) * ff_chunk)
        u = jnp.maximum(_dot(h, wu_ref[:, sl]), 0.0)
        acc = acc + _dot((u * u).astype(BF16), wd_ref[sl, :])
    o_ref[...] = _rms(acc, gf_ref[...])


def _mlp(y, g, wu_bf16, wd_bf16, g_final, bm):
    m = y.shape[0]
    row = lambda i: (i, 0)
    spec = pl.BlockSpec((bm, D_MODEL), row)
    return pl.pallas_call(
        functools.partial(_mlp_kernel, ff_chunk=1024),
        grid=(m // bm,),
        in_specs=[spec, _const_spec((1, D_MODEL)),
                  _const_spec((D_MODEL, D_FF)), _const_spec((D_FF, D_MODEL)),
                  _const_spec((1, D_MODEL))],
        out_specs=spec,
        out_shape=jax.ShapeDtypeStruct((m, D_MODEL), F32),
        compiler_params=_params(("parallel",)),
        name="mlp_final",
    )(y, g, wu_bf16, wd_bf16, g_final)


def _dattn_sample_kernel(pt_ref, q_ref, kn_ref, vn_ref, lq1, lk1, lq2, lk2, *rest,
                         pages_per_step, lam_init):
    del pt_ref
    k_refs = rest[:pages_per_step]
    v_refs = rest[pages_per_step:2 * pages_per_step]
    o_ref, qs_ref, m_ref, l_ref, acc_ref = rest[2 * pages_per_step:]
    j = pl.program_id(1)
    s_len = q_ref.shape[1]
    n_rows = 2 * DA_HEADS * s_len
    width = DA_HEADS * LANES

    @pl.when(j == 0)
    def _():
        qt = jnp.concatenate([q_ref[0]] * (2 * DA_HEADS), axis=0)
        row = lax.broadcasted_iota(jnp.int32, (n_rows, width), 0)
        col = lax.broadcasted_iota(jnp.int32, (n_rows, width), 1)
        qs = jnp.where((row >> 3) == (col >> 6), qt, 0.0).astype(BF16)
        qs_ref[...] = qs
        pad = jnp.zeros((LANES - s_len, width), F32)
        kn = jnp.concatenate([kn_ref[0], pad], axis=0).astype(BF16)
        vn = jnp.concatenate([vn_ref[0], pad], axis=0).astype(BF16)
        s = _dot_nt(qs, kn)
        r = lax.broadcasted_iota(jnp.int32, s.shape, 0) & (s_len - 1)
        c = lax.broadcasted_iota(jnp.int32, s.shape, 1)
        s = jnp.where(c <= r, s, -jnp.inf)
        m = jnp.max(s, axis=-1, keepdims=True)
        p = jnp.exp(s - m)
        m_ref[...] = m
        l_ref[...] = jnp.sum(p, axis=-1, keepdims=True)
        acc_ref[...] = _dot(p.astype(BF16), vn)

    k = jnp.concatenate([r[0].astype(BF16) for r in k_refs], axis=0)
    v = jnp.concatenate([r[0].astype(BF16) for r in v_refs], axis=0)
    _softmax_step(_dot_nt(qs_ref[...], k), v, m_ref, l_ref, acc_ref)

    @pl.when(j == pl.num_programs(1) - 1)
    def _():
        lam = _diff_lambda(lq1, lk1, lq2, lk2, lam_init)
        o = acc_ref[...] / l_ref[...]
        for h in range(DA_HEADS):
            sl = slice(h * LANES, (h + 1) * LANES)
            r0 = 2 * h * s_len
            o_ref[0, :, sl] = o[r0:r0 + s_len, sl] - lam * o[r0 + s_len:r0 + 2 * s_len, sl]


def _dattn_sample(dq, dk, dv, cache_k, cache_v, page_table, lams, lam_init, pages_per_step):
    b, s, w = dq.shape
    assert s == 8
    n_pages = page_table.shape[1]
    n_pool, page = cache_k.shape[0], cache_k.shape[1]
    assert page == LANES
    ck = cache_k.reshape(n_pool, page, w)
    cv = cache_v.reshape(n_pool, page, w)
    pt = page_table.reshape(-1)
    steps = n_pages // pages_per_step
    row = pl.BlockSpec((1, s, w), lambda bi, j, pt_r: (bi, 0, 0))
    lspec = pl.BlockSpec((1, DA_HEAD_DIM), lambda *_: (0, 0))

    def page_spec(p):
        return pl.BlockSpec(
            (1, page, w),
            lambda bi, j, pt_r: (pt_r[bi * n_pages + j * pages_per_step + p], 0, 0))

    n_rows = 2 * DA_HEADS * s
    return pl.pallas_call(
        functools.partial(_dattn_sample_kernel, pages_per_step=pages_per_step,
                          lam_init=lam_init),
        grid_spec=pltpu.PrefetchScalarGridSpec(
            num_scalar_prefetch=1,
            grid=(b, steps),
            in_specs=[row, row, row] + [lspec] * 4
                     + [page_spec(p) for p in range(pages_per_step)] * 2,
            out_specs=row,
            scratch_shapes=[pltpu.VMEM((n_rows, w), BF16),
                            pltpu.VMEM((n_rows, 1), F32),
                            pltpu.VMEM((n_rows, 1), F32),
                            pltpu.VMEM((n_rows, w), F32)]),
        out_shape=jax.ShapeDtypeStruct((b, s, w), F32),
        compiler_params=_params(("parallel", "arbitrary")),
        name="dattn_sample",
    )(pt, dq, dk, dv, *lams, *([ck] * pages_per_step), *([cv] * pages_per_step))


def kernel(x_prompt, x_sample, cache_k, cache_v, state_ret, cache_mem_k, cache_mem_v, page_table, mem_prompt, g_mix, w_in, lambda_q1, lambda_k1, lambda_q2, lambda_k2, g_diff_sub, g_ret, w_out, g_cross, g_mem, w_cq, w_ck, w_cv, w_co, g_mlp, w_up, w_down, g_final):
    bp, t, d = x_prompt.shape
    bd, s, _ = x_sample.shape
    depth = w_in.shape[0]
    assert depth == 1, "one pass over the final norm per layer stack of depth 1"
    n_mem = mem_prompt.shape[1]
    past = page_table.shape[1] * cache_k.shape[2]
    l = 0
    lam_init = 0.8 - 0.6 * math.exp(-0.3 * l)

    bm_p = 512
    tabs_p = _rope_tables(jnp.arange(t), DA_HEAD_DIM) + _rope_tables(jnp.arange(t), RET_QK_DIM)
    bm_s = 256
    pos_s = past + jnp.arange(s)
    tabs_s = tuple(jnp.tile(tb, (bm_s // s, 1)) for tb in
                   _rope_tables(pos_s, DA_HEAD_DIM) + _rope_tables(pos_s, RET_QK_DIM))

    row2 = lambda a: a.reshape(1, -1)
    bf = lambda a: a.astype(BF16)
    w_in_b, w_out_b = bf(w_in[l]), bf(w_out[l])
    w_cq_b, w_ck_b, w_cv_b, w_co_b = bf(w_cq[l]), bf(w_ck[l]), bf(w_cv[l]), bf(w_co[l])
    w_up_b, w_down_b = bf(w_up[l]), bf(w_down[l])
    lams = (row2(lambda_q1[l]), row2(lambda_k1[l]), row2(lambda_q2[l]), row2(lambda_k2[l]))
    g_mix_l, g_cross_l, g_mlp_l = row2(g_mix[l]), row2(g_cross[l]), row2(g_mlp[l])
    g_sub_l, g_ret_l, g_mem_l, g_fin = row2(g_diff_sub[l]), row2(g_ret[l]), row2(g_mem[l]), row2(g_final)

    xp = x_prompt.reshape(bp * t, d)
    dq, dk, dv, rq, rk, rv, rg = _in_proj(xp, g_mix_l, w_in_b, tabs_p, bm_p, t // bm_p)
    seq = lambda a: a.reshape(bp, t, GROUP_W)
    da = _dattn_prompt(seq(dq), seq(dk), seq(dv), lams, lam_init, 512)
    ro, s_p = _ret_prompt(seq(rq), seq(rk), seq(rv))
    y1 = _merge(da.reshape(bp * t, GROUP_W), ro.reshape(bp * t, GROUP_W), rg, xp,
                g_sub_l, g_ret_l, w_out_b, lam_init, bm_p)
    mk, mv = _mem_kv(mem_prompt.reshape(bp * n_mem, d), g_mem_l, w_ck_b, w_cv_b, 256)
    y2 = _cross(y1.reshape(bp, t, d), mk.reshape(bp, n_mem, d), mv.reshape(bp, n_mem, d),
                g_cross_l, w_cq_b, w_co_b, 512)
    y_prompt = _mlp(y2.reshape(bp * t, d), g_mlp_l, w_up_b, w_down_b, g_fin, bm_p).reshape(bp, t, d)

    xs = x_sample.reshape(bd * s, d)
    dq, dks, dvs, rq, rk, rv, rg = _in_proj(xs, g_mix_l, w_in_b, tabs_s, bm_s, 1)
    req = lambda a: a.reshape(bd, s, GROUP_W)
    da = _dattn_sample(req(dq), req(dks), req(dvs), cache_k[l], cache_v[l], page_table,
                       lams, lam_init, 16)
    ro, s_s = _ret_sample(req(rq), req(rk), req(rv), state_ret[l])
    y1 = _merge(da.reshape(bd * s, GROUP_W), ro.reshape(bd * s, GROUP_W), rg, xs,
                g_sub_l, g_ret_l, w_out_b, lam_init, bm_s)
    y2 = _cross(y1.reshape(bd, s, d), cache_mem_k[l].reshape(bd, n_mem, d),
                cache_mem_v[l].reshape(bd, n_mem, d), g_cross_l, w_cq_b, w_co_b, s)
    y_sample = _mlp(y2.reshape(bd * s, d), g_mlp_l, w_up_b, w_down_b, g_fin, bm_s).reshape(bd, s, d)

    return (y_prompt, y_sample,
            dk.reshape(1, bp, t, DA_HEADS, 2 * DA_HEAD_DIM),
            dv.reshape(1, bp, t, DA_HEADS, DA_V_DIM),
            s_p[None],
            mk.reshape(1, bp, n_mem, CA_HEADS, CA_HEAD_DIM),
            mv.reshape(1, bp, n_mem, CA_HEADS, CA_HEAD_DIM),
            dks.reshape(1, bd, s, DA_HEADS, 2 * DA_HEAD_DIM),
            dvs.reshape(1, bd, s, DA_HEADS, DA_V_DIM),
            s_s[None])
```

```python
import functools
import math

import jax
import jax.numpy as jnp
from jax import lax
from jax.experimental import pallas as pl
from jax.experimental.pallas import tpu as pltpu

F32 = jnp.float32
BF16 = jnp.bfloat16

D_MODEL = 1024
DA_HEADS = 4
DA_HEAD_DIM = 64
DA_V_DIM = 128
RET_HEADS = 4
RET_QK_DIM = 128
RET_V_DIM = 128
CA_HEADS = 4
CA_HEAD_DIM = 256
D_FF = 4096
GROUP_W = 512
N_GROUPS = 7
ROPE_THETA = 10000.0
RMS_EPS = 1e-6
RET_CHUNK = 128
LANES = 128
VMEM_LIMIT = 56 * 1024 * 1024

LOG_GAMMA = tuple(math.log(1.0 - 2.0 ** (-5.0 - h)) for h in range(RET_HEADS))


def _dot(a, b):
    return jnp.dot(a, b, preferred_element_type=F32)


def _dot_nt(a, b):
    return lax.dot_general(a, b, (((1,), (1,)), ((), ())), preferred_element_type=F32)


def _dot_tn(a, b):
    return lax.dot_general(a, b, (((0,), (0,)), ((), ())), preferred_element_type=F32)


def _rms(x, g):
    ms = jnp.mean(x * x, axis=-1, keepdims=True)
    return x * lax.rsqrt(ms + RMS_EPS) * g


def _diff_lambda(lq1, lk1, lq2, lk2, lam_init):
    a = jnp.sum(lq1[...] * lk1[...], axis=-1, keepdims=True)
    b = jnp.sum(lq2[...] * lk2[...], axis=-1, keepdims=True)
    return jnp.exp(a) - jnp.exp(b) + lam_init


def _params(sem):
    return pltpu.CompilerParams(dimension_semantics=sem, vmem_limit_bytes=VMEM_LIMIT)


def _const_spec(shape):
    nd = len(shape)
    return pl.BlockSpec(shape, lambda *_: (0,) * nd)


def _in_proj_kernel(x_ref, g_ref, w_ref, cd_ref, sd_ref, cr_ref, sr_ref,
                    kf_ref, vf_ref, qh_ref, kh_ref, vt_ref, rq_ref, rk_ref, rv_ref, rg_ref):
    h = _rms(x_ref[...], g_ref[...]).astype(BF16)
    bm = h.shape[0]
    lane = lax.broadcasted_iota(jnp.int32, (bm, LANES), 1)
    first_half = (lane & (DA_HEAD_DIM - 1)) < (DA_HEAD_DIM // 2)
    cd, sd, cr, sr = cd_ref[...], sd_ref[...], cr_ref[...], sr_ref[...]

    def proj(g):
        return _dot(h, w_ref[:, g * GROUP_W:(g + 1) * GROUP_W])

    def rope_da(x):
        sw = jnp.where(first_half, pltpu.roll(x, LANES - 32, 1), pltpu.roll(x, 32, 1))
        return x * cd + sw * sd

    def rope_ret(x):
        return x * cr + pltpu.roll(x, 64, 1) * sr

    def store(ref, x, fn):
        for c in range(GROUP_W // LANES):
            sl = slice(c * LANES, (c + 1) * LANES)
            ref[:, sl] = fn(x[:, sl])

    dq, dk, dv = proj(0), proj(1), proj(2)
    for hd in range(DA_HEADS):
        sl = slice(hd * LANES, (hd + 1) * LANES)
        rows = pl.ds(hd, bm, stride=DA_HEADS)
        qh_ref[hd] = (rope_da(dq[:, sl]) * (DA_HEAD_DIM ** -0.5)).astype(BF16)
        k = rope_da(dk[:, sl])
        kf_ref[rows, :] = k
        kh_ref[hd] = k.astype(BF16)
        v = dv[:, sl]
        vf_ref[rows, :] = v
        vt_ref[hd, 0] = v.T.astype(BF16)
    store(rq_ref, proj(3), rope_ret)
    store(rk_ref, proj(4), lambda x: rope_ret(x) * (RET_QK_DIM ** -0.5))
    rv_ref[...] = proj(5)
    rg_ref[...] = proj(6)


def _in_proj(x, g, w_bf16, tabs, bm, tab_blocks):
    m = x.shape[0]
    row = lambda i: (i, 0)
    tab = lambda i: (i % tab_blocks, 0)
    wide = pl.BlockSpec((bm, GROUP_W), row)
    inter = pl.BlockSpec((bm * DA_HEADS, LANES), row)
    head = pl.BlockSpec((DA_HEADS, bm, LANES), lambda i: (0, i, 0))
    head_t = pl.BlockSpec((DA_HEADS, 1, LANES, bm), lambda i: (0, i, 0, 0))
    return pl.pallas_call(
        _in_proj_kernel,
        grid=(m // bm,),
        in_specs=[pl.BlockSpec((bm, D_MODEL), row),
                  _const_spec((1, D_MODEL)),
                  _const_spec((D_MODEL, N_GROUPS * GROUP_W))]
                 + [pl.BlockSpec((bm, LANES), tab)] * 4,
        out_specs=[inter, inter, head, head, head_t, wide, wide, wide, wide],
        out_shape=[jax.ShapeDtypeStruct((m * DA_HEADS, LANES), F32)] * 2
                  + [jax.ShapeDtypeStruct((DA_HEADS, m, LANES), BF16)] * 2
                  + [jax.ShapeDtypeStruct((DA_HEADS, m // bm, LANES, bm), BF16)]
                  + [jax.ShapeDtypeStruct((m, GROUP_W), F32)] * 4,
        compiler_params=_params(("parallel",)),
        name="in_proj",
    )(x, g, w_bf16, *tabs)


def _rope_tables(pos, dim):
    inv = 1.0 / (ROPE_THETA ** (jnp.arange(0, dim, 2, dtype=F32) / dim))
    ang = pos.astype(F32)[:, None] * inv[None, :]
    c, s = jnp.cos(ang), jnp.sin(ang)
    reps = LANES // dim
    return (jnp.tile(jnp.concatenate([c, c], axis=-1), (1, reps)),
            jnp.tile(jnp.concatenate([-s, s], axis=-1), (1, reps)))


def _softmax_step(s, v, m_ref, l_ref, acc_ref):
    m_old = m_ref[...]
    m_new = jnp.maximum(m_old, jnp.max(s, axis=-1, keepdims=True))
    alpha = jnp.exp(m_old - m_new)
    p = jnp.exp(s - m_new)
    l_ref[...] = alpha * l_ref[...] + jnp.sum(p, axis=-1, keepdims=True)
    acc_ref[...] = alpha * acc_ref[...] + _dot(p.astype(BF16), v)
    m_ref[...] = m_new


def _dattn_prompt_kernel(q_ref, k_ref, vt_ref, lq1, lk1, lq2, lk2,
                         o_ref, m_ref, l_ref, acc_ref, *, lam_init):
    qi = pl.program_id(2)
    bq = q_ref.shape[0]
    bk = vt_ref.shape[2]

    qt = q_ref[...].astype(F32).T
    row = lax.broadcasted_iota(jnp.int32, qt.shape, 0)
    qs_t = jnp.concatenate([jnp.where(row < DA_HEAD_DIM, qt, 0.0),
                            jnp.where(row >= DA_HEAD_DIM, qt, 0.0)], axis=1).astype(BF16)
    m_ref[...] = jnp.full(m_ref.shape, -jnp.inf, F32)
    l_ref[...] = jnp.zeros(l_ref.shape, F32)
    acc_ref[...] = jnp.zeros(acc_ref.shape, F32)

    def block(kj, diagonal):
        k = k_ref[pl.ds(pl.multiple_of(kj * bk, bk), bk), :]
        s = _dot(k, qs_t)
        if diagonal:
            kpos = lax.broadcasted_iota(jnp.int32, s.shape, 0)
            qpos = lax.broadcasted_iota(jnp.int32, s.shape, 1) & (bq - 1)
            s = jnp.where(kpos <= qpos, s, -jnp.inf)
        m_old = m_ref[...]
        m_new = jnp.maximum(m_old, jnp.max(s, axis=0, keepdims=True))
        alpha = jnp.exp(m_old - m_new)
        p = jnp.exp(s - m_new)
        l_ref[...] = alpha * l_ref[...] + jnp.sum(p, axis=0, keepdims=True)
        acc_ref[...] = alpha * acc_ref[...] + _dot(vt_ref[kj], p.astype(BF16))
        m_ref[...] = m_new

    def body(kj, carry):
        block(kj, False)
        return carry

    lax.fori_loop(0, qi, body, 0)
    block(qi, True)

    lam = _diff_lambda(lq1, lk1, lq2, lk2, lam_init)
    o = acc_ref[...] / l_ref[...]
    o_ref[...] = (o[:, :bq] - lam * o[:, bq:]).T


def _dattn_prompt(qh, kh, vt, lams, lam_init, b, t):
    blk = vt.shape[3]
    nq = t // blk
    assert blk & (blk - 1) == 0
    lspec = pl.BlockSpec((1, DA_HEAD_DIM), lambda *_: (0, 0))
    return pl.pallas_call(
        functools.partial(_dattn_prompt_kernel, lam_init=lam_init),
        grid=(b, DA_HEADS, nq),
        in_specs=[pl.BlockSpec((None, blk, LANES), lambda bi, h, qi: (h, bi * nq + qi, 0)),
                  pl.BlockSpec((None, t, LANES), lambda bi, h, qi: (h, bi, 0)),
                  pl.BlockSpec((None, nq, LANES, blk), lambda bi, h, qi: (h, bi, 0, 0))]
                 + [lspec] * 4,
        out_specs=pl.BlockSpec((blk, LANES), lambda bi, h, qi: (bi * nq + qi, h)),
        out_shape=jax.ShapeDtypeStruct((b * t, DA_HEADS * DA_V_DIM), F32),
        scratch_shapes=[pltpu.VMEM((1, 2 * blk), F32),
                        pltpu.VMEM((1, 2 * blk), F32),
                        pltpu.VMEM((LANES, 2 * blk), F32)],
        compiler_params=_params(("parallel", "parallel", "arbitrary")),
        name="dattn_prompt",
    )(qh, kh, vt, *lams)


def _ret_chunk(q, k, v, state, lg, c_len):
    cq, ck = q.shape[0], k.shape[0]
    i = lax.broadcasted_iota(jnp.int32, (cq, ck), 0)
    j = lax.broadcasted_iota(jnp.int32, (cq, ck), 1)
    diff = (i - j).astype(F32)
    decay = jnp.where(diff >= 0, jnp.exp(jnp.maximum(diff, 0.0) * lg), 0.0)
    qb, vb = q.astype(BF16), v.astype(BF16)
    inner = _dot_nt(qb, k.astype(BF16)) * decay
    o = _dot(inner.astype(BF16), vb)
    iq = lax.broadcasted_iota(jnp.int32, (cq, 1), 0).astype(F32)
    o = o + _dot(qb, state.astype(BF16)) * jnp.exp((iq + 1.0) * lg)
    ik = lax.broadcasted_iota(jnp.int32, (ck, 1), 0).astype(F32)
    k_decay = jnp.where(ik < c_len, jnp.exp((c_len - 1.0 - ik) * lg), 0.0)
    new_state = math.exp(c_len * lg) * state + _dot_tn((k * k_decay).astype(BF16), vb)
    return o, new_state


def _ret_prompt_kernel(q_ref, k_ref, v_ref, o_ref, s_ref):
    @pl.when(pl.program_id(1) == 0)
    def _():
        s_ref[...] = jnp.zeros(s_ref.shape, F32)

    for h in range(RET_HEADS):
        sl = slice(h * LANES, (h + 1) * LANES)
        o, s_new = _ret_chunk(q_ref[0, :, sl], k_ref[0, :, sl], v_ref[0, :, sl],
                              s_ref[0, h], LOG_GAMMA[h], RET_CHUNK)
        o_ref[0, :, sl] = o
        s_ref[0, h] = s_new


def _ret_prompt(rq, rk, rv):
    b, t, w = rq.shape
    spec = pl.BlockSpec((1, RET_CHUNK, w), lambda bi, c: (bi, c, 0))
    return pl.pallas_call(
        _ret_prompt_kernel,
        grid=(b, t // RET_CHUNK),
        in_specs=[spec] * 3,
        out_specs=[spec, pl.BlockSpec((1, RET_HEADS, RET_QK_DIM, RET_V_DIM),
                                      lambda bi, c: (bi, 0, 0, 0))],
        out_shape=[jax.ShapeDtypeStruct((b, t, w), F32),
                   jax.ShapeDtypeStruct((b, RET_HEADS, RET_QK_DIM, RET_V_DIM), F32)],
        compiler_params=_params(("parallel", "arbitrary")),
        name="ret_prompt",
    )(rq, rk, rv)


def _ret_sample_kernel(q_ref, k_ref, v_ref, s_ref, o_ref, sn_ref):
    c_len = q_ref.shape[1]
    pad = jnp.zeros((LANES - c_len, LANES), F32)
    for h in range(RET_HEADS):
        sl = slice(h * LANES, (h + 1) * LANES)
        k = jnp.concatenate([k_ref[0, :, sl], pad], axis=0)
        v = jnp.concatenate([v_ref[0, :, sl], pad], axis=0)
        o, s_new = _ret_chunk(q_ref[0, :, sl], k, v, s_ref[0, h], LOG_GAMMA[h], c_len)
        o_ref[0, :, sl] = o
        sn_ref[0, h] = s_new


def _ret_sample(rq, rk, rv, state):
    b, s, w = rq.shape
    spec = pl.BlockSpec((1, s, w), lambda bi: (bi, 0, 0))
    sspec = pl.BlockSpec((1, RET_HEADS, RET_QK_DIM, RET_V_DIM), lambda bi: (bi, 0, 0, 0))
    return pl.pallas_call(
        _ret_sample_kernel,
        grid=(b,),
        in_specs=[spec] * 3 + [sspec],
        out_specs=[spec, sspec],
        out_shape=[jax.ShapeDtypeStruct((b, s, w), F32),
                   jax.ShapeDtypeStruct(state.shape, F32)],
        compiler_params=_params(("parallel",)),
        name="ret_sample",
    )(rq, rk, rv, state)


def _merge_kernel(da_ref, ro_ref, rg_ref, x_ref, gs_ref, gr_ref, w_ref, y_ref, *, lam_init):
    parts = []
    for h in range(DA_HEADS):
        sl = slice(h * LANES, (h + 1) * LANES)
        parts.append(_rms(da_ref[:, sl], gs_ref[...]) * (1.0 - lam_init))
    for h in range(RET_HEADS):
        sl = slice(h * LANES, (h + 1) * LANES)
        g = rg_ref[:, sl]
        parts.append(_rms(ro_ref[:, sl], gr_ref[...]) * (g * (1.0 / (1.0 + jnp.exp(-g)))))
    mix = jnp.concatenate(parts, axis=-1).astype(BF16)
    y_ref[...] = x_ref[...] + _dot(mix, w_ref[...])


def _merge(da, ro, rg, x, g_sub, g_ret, w_bf16, lam_init, bm):
    m = x.shape[0]
    row = lambda i: (i, 0)
    return pl.pallas_call(
        functools.partial(_merge_kernel, lam_init=lam_init),
        grid=(m // bm,),
        in_specs=[pl.BlockSpec((bm, GROUP_W), row)] * 3
                 + [pl.BlockSpec((bm, D_MODEL), row),
                    _const_spec((1, LANES)), _const_spec((1, LANES)),
                    _const_spec((D_MODEL, D_MODEL))],
        out_specs=pl.BlockSpec((bm, D_MODEL), row),
        out_shape=jax.ShapeDtypeStruct((m, D_MODEL), F32),
        compiler_params=_params(("parallel",)),
        name="merge_out",
    )(da, ro, rg, x, g_sub, g_ret, w_bf16)


def _mem_kv_kernel(x_ref, g_ref, wk_ref, wv_ref, k_ref, v_ref):
    h = _rms(x_ref[...], g_ref[...]).astype(BF16)
    k = _dot(h, wk_ref[...])
    v = _dot(h, wv_ref[...])
    for hd in range(CA_HEADS):
        sl = slice(hd * CA_HEAD_DIM, (hd + 1) * CA_HEAD_DIM)
        k_ref[:, hd, :] = k[:, sl]
        v_ref[:, hd, :] = v[:, sl]


def _mem_kv(x, g, wk_bf16, wv_bf16, bm):
    m = x.shape[0]
    spec = pl.BlockSpec((bm, D_MODEL), lambda i: (i, 0))
    ospec = pl.BlockSpec((bm, CA_HEADS, CA_HEAD_DIM), lambda i: (i, 0, 0))
    return pl.pallas_call(
        _mem_kv_kernel,
        grid=(m // bm,),
        in_specs=[spec, _const_spec((1, D_MODEL)),
                  _const_spec((D_MODEL, D_MODEL)), _const_spec((D_MODEL, D_MODEL))],
        out_specs=[ospec, ospec],
        out_shape=[jax.ShapeDtypeStruct((m, CA_HEADS, CA_HEAD_DIM), F32)] * 2,
        compiler_params=_params(("parallel",)),
        name="mem_kv",
    )(x, g, wk_bf16, wv_bf16)


def _cross_kernel(y_ref, mk_ref, mv_ref, g_ref, wq_ref, wo_ref, o_ref):
    y = y_ref[0]
    h = _rms(y, g_ref[...]).astype(BF16)
    q = (_dot(h, wq_ref[...]) * (CA_HEAD_DIM ** -0.5)).astype(BF16)
    outs = []
    for hd in range(CA_HEADS):
        sl = slice(hd * CA_HEAD_DIM, (hd + 1) * CA_HEAD_DIM)
        s = _dot_nt(q[:, sl], mk_ref[:, hd, :].astype(BF16))
        p = jnp.exp(s - jnp.max(s, axis=-1, keepdims=True))
        a = p / jnp.sum(p, axis=-1, keepdims=True)
        outs.append(_dot(a.astype(BF16), mv_ref[:, hd, :].astype(BF16)))
    o = jnp.concatenate(outs, axis=-1).astype(BF16)
    o_ref[0] = y + _dot(o, wo_ref[...])


def _cross(y, mk, mv, layer, g, wq_bf16, wo_bf16, bt):
    b, t, _ = y.shape
    n_mem = mk.shape[2]
    yspec = pl.BlockSpec((1, bt, D_MODEL), lambda bi, ti: (bi, ti, 0))
    mspec = pl.BlockSpec((None, None, n_mem, CA_HEADS, CA_HEAD_DIM),
                         lambda bi, ti: (layer, bi, 0, 0, 0))
    return pl.pallas_call(
        _cross_kernel,
        grid=(b, t // bt),
        in_specs=[yspec, mspec, mspec, _const_spec((1, D_MODEL)),
                  _const_spec((D_MODEL, D_MODEL)), _const_spec((D_MODEL, D_MODEL))],
        out_specs=yspec,
        out_shape=jax.ShapeDtypeStruct(y.shape, F32),
        compiler_params=_params(("parallel", "arbitrary")),
        name="cross_attn",
    )(y, mk, mv, g, wq_bf16, wo_bf16)


def _mlp_kernel(y_ref, g_ref, wu_ref, wd_ref, gf_ref, o_ref, *, ff_chunk):
    y = y_ref[...]
    h = _rms(y, g_ref[...]).astype(BF16)
    acc = y
    for c in range(D_FF // ff_chunk):
        sl = slice(c * ff_chunk, (c + 1) * ff_chunk)
        u = jnp.maximum(_dot(h, wu_ref[:, sl]), 0.0)
        acc = acc + _dot((u * u).astype(BF16), wd_ref[sl, :])
    o_ref[...] = _rms(acc, gf_ref[...])


def _mlp(y, g, wu_bf16, wd_bf16, g_final, bm):
    m = y.shape[0]
    row = lambda i: (i, 0)
    spec = pl.BlockSpec((bm, D_MODEL), row)
    return pl.pallas_call(
        functools.partial(_mlp_kernel, ff_chunk=1024),
        grid=(m // bm,),
        in_specs=[spec, _const_spec((1, D_MODEL)),
                  _const_spec((D_MODEL, D_FF)), _const_spec((D_FF, D_MODEL)),
                  _const_spec((1, D_MODEL))],
        out_specs=spec,
        out_shape=jax.ShapeDtypeStruct((m, D_MODEL), F32),
        compiler_params=_params(("parallel",)),
        name="mlp_final",
    )(y, g, wu_bf16, wd_bf16, g_final)


def _dattn_sample_kernel(pt_ref, q_ref, kn_ref, vn_ref, lq1, lk1, lq2, lk2, *rest,
                         pages_per_step, lam_init):
    del pt_ref
    k_refs = rest[:pages_per_step]
    v_refs = rest[pages_per_step:2 * pages_per_step]
    o_ref, qs_ref, m_ref, l_ref, acc_ref = rest[2 * pages_per_step:]
    j = pl.program_id(1)
    s_len = q_ref.shape[1]
    n_rows = 2 * DA_HEADS * s_len
    page_rows = k_refs[0].shape[0]

    def same_head(r, c):
        return (c & (DA_HEADS - 1)) == (r >> 4)

    @pl.when(j == 0)
    def _():
        parts = []
        lane = lax.broadcasted_iota(jnp.int32, (s_len, LANES), 1)
        for hd in range(DA_HEADS):
            q = q_ref[hd].astype(F32)
            parts += [jnp.where(lane < DA_HEAD_DIM, q, 0.0), jnp.where(lane >= DA_HEAD_DIM, q, 0.0)]
        qs = jnp.concatenate(parts, axis=0).astype(BF16)
        qs_ref[...] = qs
        pad = jnp.zeros((LANES - DA_HEADS * s_len, LANES), F32)
        kn = jnp.concatenate([kn_ref[...], pad], axis=0).astype(BF16)
        vn = jnp.concatenate([vn_ref[...], pad], axis=0).astype(BF16)
        s = _dot_nt(qs, kn)
        r = lax.broadcasted_iota(jnp.int32, s.shape, 0)
        c = lax.broadcasted_iota(jnp.int32, s.shape, 1)
        causal = (c >> 2) <= (r & (s_len - 1))
        s = jnp.where(same_head(r, c) & causal, s, -jnp.inf)
        m = jnp.max(s, axis=-1, keepdims=True)
        p = jnp.exp(s - m)
        m_ref[...] = m
        l_ref[...] = jnp.sum(p, axis=-1, keepdims=True)
        acc_ref[...] = _dot(p.astype(BF16), vn)

    k = jnp.concatenate([r[...].astype(BF16) for r in k_refs], axis=0)
    v = jnp.concatenate([r[...].astype(BF16) for r in v_refs], axis=0)
    r = lax.broadcasted_iota(jnp.int32, (n_rows, LANES), 0)
    c = lax.broadcasted_iota(jnp.int32, (n_rows, LANES), 1)
    bias = jnp.where(same_head(r, c), 0.0, -jnp.inf)
    bias = jnp.concatenate([bias] * (pages_per_step * page_rows // LANES), axis=1)
    _softmax_step(_dot_nt(qs_ref[...], k) + bias, v, m_ref, l_ref, acc_ref)

    @pl.when(j == pl.num_programs(1) - 1)
    def _():
        lam = _diff_lambda(lq1, lk1, lq2, lk2, lam_init)
        o = acc_ref[...] / l_ref[...]
        for h in range(DA_HEADS):
            r0 = 2 * h * s_len
            o_ref[:, h * LANES:(h + 1) * LANES] = (o[r0:r0 + s_len]
                                                   - lam * o[r0 + s_len:r0 + 2 * s_len])


def _dattn_sample(qh, kf, vf, cache_k, cache_v, layer, page_table, lams, lam_init, pages_per_step):
    b, n_pages = page_table.shape
    s = qh.shape[1] // b
    assert s == 8 and DA_HEADS == 4
    depth, n_pool, page = cache_k.shape[:3]
    page_rows = page * DA_HEADS
    ck = cache_k.reshape(depth, n_pool, page_rows, LANES)
    cv = cache_v.reshape(depth, n_pool, page_rows, LANES)
    pt = page_table.reshape(-1)
    steps = n_pages // pages_per_step
    qspec = pl.BlockSpec((DA_HEADS, s, LANES), lambda bi, j, pt_r: (0, bi, 0))
    nspec = pl.BlockSpec((s * DA_HEADS, LANES), lambda bi, j, pt_r: (bi, 0))
    lspec = pl.BlockSpec((1, DA_HEAD_DIM), lambda *_: (0, 0))

    def page_spec(p):
        return pl.BlockSpec(
            (None, None, page_rows, LANES),
            lambda bi, j, pt_r: (layer, pt_r[bi * n_pages + j * pages_per_step + p], 0, 0))

    n_rows = 2 * DA_HEADS * s
    return pl.pallas_call(
        functools.partial(_dattn_sample_kernel, pages_per_step=pages_per_step,
                          lam_init=lam_init),
        grid_spec=pltpu.PrefetchScalarGridSpec(
            num_scalar_prefetch=1,
            grid=(b, steps),
            in_specs=[qspec, nspec, nspec] + [lspec] * 4
                     + [page_spec(p) for p in range(pages_per_step)] * 2,
            out_specs=pl.BlockSpec((s, DA_HEADS * LANES), lambda bi, j, pt_r: (bi, 0)),
            scratch_shapes=[pltpu.VMEM((n_rows, LANES), BF16),
                            pltpu.VMEM((n_rows, 1), F32),
                            pltpu.VMEM((n_rows, 1), F32),
                            pltpu.VMEM((n_rows, LANES), F32)]),
        out_shape=jax.ShapeDtypeStruct((b * s, DA_HEADS * LANES), F32),
        compiler_params=_params(("parallel", "arbitrary")),
        name="dattn_sample",
    )(pt, qh, kf, vf, *lams, *([ck] * pages_per_step), *([cv] * pages_per_step))


def kernel(x_prompt, x_sample, cache_k, cache_v, state_ret, cache_mem_k, cache_mem_v, page_table, mem_prompt, g_mix, w_in, lambda_q1, lambda_k1, lambda_q2, lambda_k2, g_diff_sub, g_ret, w_out, g_cross, g_mem, w_cq, w_ck, w_cv, w_co, g_mlp, w_up, w_down, g_final):
    bp, t, d = x_prompt.shape
    bd, s, _ = x_sample.shape
    depth = w_in.shape[0]
    assert depth == 1, "one pass over the final norm per layer stack of depth 1"
    n_mem = mem_prompt.shape[1]
    past = page_table.shape[1] * cache_k.shape[2]
    l = 0
    lam_init = 0.8 - 0.6 * math.exp(-0.3 * l)

    bm_p = 512
    tabs_p = _rope_tables(jnp.arange(t), DA_HEAD_DIM) + _rope_tables(jnp.arange(t), RET_QK_DIM)
    bm_s = min(256, bd * s)
    pos_s = past + jnp.arange(s)
    tabs_s = tuple(jnp.tile(tb, (bm_s // s, 1)) for tb in
                   _rope_tables(pos_s, DA_HEAD_DIM) + _rope_tables(pos_s, RET_QK_DIM))

    row2 = lambda a: a.reshape(1, -1)
    bf = lambda a: a.astype(BF16)
    w_in_b, w_out_b = bf(w_in[l]), bf(w_out[l])
    w_cq_b, w_ck_b, w_cv_b, w_co_b = bf(w_cq[l]), bf(w_ck[l]), bf(w_cv[l]), bf(w_co[l])
    w_up_b, w_down_b = bf(w_up[l]), bf(w_down[l])
    lams = (row2(lambda_q1[l]), row2(lambda_k1[l]), row2(lambda_q2[l]), row2(lambda_k2[l]))
    g_mix_l, g_cross_l, g_mlp_l = row2(g_mix[l]), row2(g_cross[l]), row2(g_mlp[l])
    g_sub_l, g_ret_l, g_mem_l, g_fin = row2(g_diff_sub[l]), row2(g_ret[l]), row2(g_mem[l]), row2(g_final)

    xp = x_prompt.reshape(bp * t, d)
    kf, vf, qh, kh, vt, rq, rk, rv, rg = _in_proj(xp, g_mix_l, w_in_b, tabs_p, bm_p, t // bm_p)
    seq = lambda a: a.reshape(bp, t, GROUP_W)
    da = _dattn_prompt(qh, kh, vt, lams, lam_init, bp, t)
    ro, s_p = _ret_prompt(seq(rq), seq(rk), seq(rv))
    y1 = _merge(da, ro.reshape(bp * t, GROUP_W), rg, xp,
                g_sub_l, g_ret_l, w_out_b, lam_init, bm_p)
    mk, mv = _mem_kv(mem_prompt.reshape(bp * n_mem, d), g_mem_l, w_ck_b, w_cv_b, 256)
    mk = mk.reshape(1, bp, n_mem, CA_HEADS, CA_HEAD_DIM)
    mv = mv.reshape(1, bp, n_mem, CA_HEADS, CA_HEAD_DIM)
    y2 = _cross(y1.reshape(bp, t, d), mk, mv, 0, g_cross_l, w_cq_b, w_co_b, 512)
    y_prompt = _mlp(y2.reshape(bp * t, d), g_mlp_l, w_up_b, w_down_b, g_fin, bm_p).reshape(bp, t, d)

    xs = x_sample.reshape(bd * s, d)
    kfs, vfs, qh, _, _, rq, rk, rv, rg = _in_proj(xs, g_mix_l, w_in_b, tabs_s, bm_s, 1)
    req = lambda a: a.reshape(bd, s, GROUP_W)
    da = _dattn_sample(qh, kfs, vfs, cache_k, cache_v, l, page_table, lams, lam_init, 16)
    ro, s_s = _ret_sample(req(rq), req(rk), req(rv), state_ret[l])
    y1 = _merge(da, ro.reshape(bd * s, GROUP_W), rg, xs,
                g_sub_l, g_ret_l, w_out_b, lam_init, bm_s)
    y2 = _cross(y1.reshape(bd, s, d), cache_mem_k, cache_mem_v, l, g_cross_l, w_cq_b, w_co_b, s)
    y_sample = _mlp(y2.reshape(bd * s, d), g_mlp_l, w_up_b, w_down_b, g_fin, bm_s).reshape(bd, s, d)

    return (y_prompt, y_sample,
            kf.reshape(1, bp, t, DA_HEADS, 2 * DA_HEAD_DIM),
            vf.reshape(1, bp, t, DA_HEADS, DA_V_DIM),
            s_p[None], mk, mv,
            kfs.reshape(1, bd, s, DA_HEADS, 2 * DA_HEAD_DIM),
            vfs.reshape(1, bd, s, DA_HEADS, DA_V_DIM),
            s_s[None])
```

```python
import functools
import math

import jax
import jax.numpy as jnp
from jax import lax
from jax.experimental import pallas as pl
from jax.experimental.pallas import tpu as pltpu

F32 = jnp.float32
BF16 = jnp.bfloat16

D_MODEL = 1024
DA_HEADS = 4
DA_HEAD_DIM = 64
DA_V_DIM = 128
RET_HEADS = 4
RET_QK_DIM = 128
RET_V_DIM = 128
CA_HEADS = 4
CA_HEAD_DIM = 256
D_FF = 4096
GROUP_W = 512
N_GROUPS = 7
ROPE_THETA = 10000.0
RMS_EPS = 1e-6
RET_CHUNK = 128
LANES = 128
VMEM_LIMIT = 56 * 1024 * 1024

LOG_GAMMA = tuple(math.log(1.0 - 2.0 ** (-5.0 - h)) for h in range(RET_HEADS))


def _dot(a, b):
    return jnp.dot(a, b, preferred_element_type=F32)


def _dot_nt(a, b):
    return lax.dot_general(a, b, (((1,), (1,)), ((), ())), preferred_element_type=F32)


def _dot_tn(a, b):
    return lax.dot_general(a, b, (((0,), (0,)), ((), ())), preferred_element_type=F32)


def _rms(x, g):
    ms = jnp.mean(x * x, axis=-1, keepdims=True)
    return x * lax.rsqrt(ms + RMS_EPS) * g


def _diff_lambda(lq1, lk1, lq2, lk2, lam_init):
    a = jnp.sum(lq1[...] * lk1[...], axis=-1, keepdims=True)
    b = jnp.sum(lq2[...] * lk2[...], axis=-1, keepdims=True)
    return jnp.exp(a) - jnp.exp(b) + lam_init


def _params(sem):
    return pltpu.CompilerParams(dimension_semantics=sem, vmem_limit_bytes=VMEM_LIMIT)


def _const_spec(shape):
    nd = len(shape)
    return pl.BlockSpec(shape, lambda *_: (0,) * nd, pipeline_mode=pl.Buffered(1))


def _in_proj_kernel(x_ref, g_ref, w_ref, cd_ref, sd_ref, cr_ref, sr_ref,
                    kf_ref, vf_ref, qh_ref, kh_ref, vt_ref, rq_ref, rk_ref, rv_ref, rg_ref):
    h = _rms(x_ref[...], g_ref[...]).astype(BF16)
    bm = h.shape[0]
    lane = lax.broadcasted_iota(jnp.int32, (bm, LANES), 1)
    first_half = (lane & (DA_HEAD_DIM - 1)) < (DA_HEAD_DIM // 2)
    cd, sd, cr, sr = cd_ref[...], sd_ref[...], cr_ref[...], sr_ref[...]

    def proj(g):
        return _dot(h, w_ref[:, g * GROUP_W:(g + 1) * GROUP_W])

    def rope_da(x):
        sw = jnp.where(first_half, pltpu.roll(x, LANES - 32, 1), pltpu.roll(x, 32, 1))
        return x * cd + sw * sd

    def rope_ret(x):
        return x * cr + pltpu.roll(x, 64, 1) * sr

    def store(ref, x, fn):
        for c in range(GROUP_W // LANES):
            sl = slice(c * LANES, (c + 1) * LANES)
            ref[:, sl] = fn(x[:, sl])

    dq, dk, dv = proj(0), proj(1), proj(2)
    for hd in range(DA_HEADS):
        sl = slice(hd * LANES, (hd + 1) * LANES)
        rows = pl.ds(hd, bm, stride=DA_HEADS)
        qh_ref[hd] = (rope_da(dq[:, sl]) * (DA_HEAD_DIM ** -0.5)).astype(BF16)
        k = rope_da(dk[:, sl])
        kf_ref[rows, :] = k
        kh_ref[hd] = k.astype(BF16)
        v = dv[:, sl]
        vf_ref[rows, :] = v
        vt_ref[hd, 0] = v.T.astype(BF16)
    store(rq_ref, proj(3), rope_ret)
    store(rk_ref, proj(4), lambda x: rope_ret(x) * (RET_QK_DIM ** -0.5))
    rv_ref[...] = proj(5)
    rg_ref[...] = proj(6)


def _in_proj(x, g, w_bf16, tabs, bm, tab_blocks):
    m = x.shape[0]
    row = lambda i: (i, 0)
    tab = lambda i: (i % tab_blocks, 0)
    wide = pl.BlockSpec((bm, GROUP_W), row)
    inter = pl.BlockSpec((bm * DA_HEADS, LANES), row)
    head = pl.BlockSpec((DA_HEADS, bm, LANES), lambda i: (0, i, 0))
    head_t = pl.BlockSpec((DA_HEADS, 1, LANES, bm), lambda i: (0, i, 0, 0))
    return pl.pallas_call(
        _in_proj_kernel,
        grid=(m // bm,),
        in_specs=[pl.BlockSpec((bm, D_MODEL), row),
                  _const_spec((1, D_MODEL)),
                  _const_spec((D_MODEL, N_GROUPS * GROUP_W))]
                 + [pl.BlockSpec((bm, LANES), tab)] * 4,
        out_specs=[inter, inter, head, head, head_t, wide, wide, wide, wide],
        out_shape=[jax.ShapeDtypeStruct((m * DA_HEADS, LANES), F32)] * 2
                  + [jax.ShapeDtypeStruct((DA_HEADS, m, LANES), BF16)] * 2
                  + [jax.ShapeDtypeStruct((DA_HEADS, m // bm, LANES, bm), BF16)]
                  + [jax.ShapeDtypeStruct((m, GROUP_W), F32)] * 4,
        compiler_params=_params(("parallel",)),
        name="in_proj",
    )(x, g, w_bf16, *tabs)


def _rope_tables(pos, dim):
    inv = 1.0 / (ROPE_THETA ** (jnp.arange(0, dim, 2, dtype=F32) / dim))
    ang = pos.astype(F32)[:, None] * inv[None, :]
    c, s = jnp.cos(ang), jnp.sin(ang)
    reps = LANES // dim
    return (jnp.tile(jnp.concatenate([c, c], axis=-1), (1, reps)),
            jnp.tile(jnp.concatenate([-s, s], axis=-1), (1, reps)))


def _dattn_prompt_kernel(q_ref, k_ref, vt_ref, lq1, lk1, lq2, lk2,
                         o_ref, qs_ref, sa_ref, sb_ref, m_ref, l_ref, acc_ref, *, lam_init):
    qi = pl.program_id(2)
    bq = q_ref.shape[0]
    bk = vt_ref.shape[2]

    qt = q_ref[...].astype(F32).T
    row = lax.broadcasted_iota(jnp.int32, qt.shape, 0)
    qs_ref[...] = jnp.concatenate([jnp.where(row < DA_HEAD_DIM, qt, 0.0),
                                   jnp.where(row >= DA_HEAD_DIM, qt, 0.0)], axis=1).astype(BF16)
    m_ref[...] = jnp.full(m_ref.shape, -jnp.inf, F32)
    l_ref[...] = jnp.zeros(l_ref.shape, F32)
    acc_ref[...] = jnp.zeros(acc_ref.shape, F32)

    def scores(kj):
        return _dot(k_ref[pl.ds(pl.multiple_of(kj * bk, bk), bk), :], qs_ref[...])

    def consume(s_ref, kj, diagonal):
        s = s_ref[...]
        if diagonal:
            kpos = lax.broadcasted_iota(jnp.int32, s.shape, 0)
            qpos = lax.broadcasted_iota(jnp.int32, s.shape, 1) & (bq - 1)
            s = jnp.where(kpos <= qpos, s, -jnp.inf)
        m_old = m_ref[...]
        m_new = jnp.maximum(m_old, jnp.max(s, axis=0, keepdims=True))
        alpha = jnp.exp(m_old - m_new)
        p = jnp.exp(s - m_new)
        l_ref[...] = alpha * l_ref[...] + jnp.sum(p, axis=0, keepdims=True)
        acc_ref[...] = alpha * acc_ref[...] + _dot(vt_ref[kj], p.astype(BF16))
        m_ref[...] = m_new

    sa_ref[...] = scores(0)

    def pair(i, carry):
        kj = 2 * i
        sb_ref[...] = scores(kj + 1)
        consume(sa_ref, kj, False)
        sa_ref[...] = scores(kj + 2)
        consume(sb_ref, kj + 1, False)
        return carry

    lax.fori_loop(0, qi >> 1, pair, 0)

    @pl.when((qi & 1) == 1)
    def _():
        sb_ref[...] = scores(qi)
        consume(sa_ref, qi - 1, False)
        consume(sb_ref, qi, True)

    @pl.when((qi & 1) == 0)
    def _():
        consume(sa_ref, qi, True)

    lam = _diff_lambda(lq1, lk1, lq2, lk2, lam_init)
    o = acc_ref[...] / l_ref[...]
    o_ref[...] = (o[:, :bq] - lam * o[:, bq:]).T


def _dattn_prompt(qh, kh, vt, lams, lam_init, b, t):
    blk = vt.shape[3]
    nq = t // blk
    assert blk & (blk - 1) == 0
    lspec = pl.BlockSpec((1, DA_HEAD_DIM), lambda *_: (0, 0))
    return pl.pallas_call(
        functools.partial(_dattn_prompt_kernel, lam_init=lam_init),
        grid=(b, DA_HEADS, nq),
        in_specs=[pl.BlockSpec((None, blk, LANES), lambda bi, h, qi: (h, bi * nq + qi, 0)),
                  pl.BlockSpec((None, t, LANES), lambda bi, h, qi: (h, bi, 0)),
                  pl.BlockSpec((None, nq, LANES, blk), lambda bi, h, qi: (h, bi, 0, 0))]
                 + [lspec] * 4,
        out_specs=pl.BlockSpec((blk, LANES), lambda bi, h, qi: (bi * nq + qi, h)),
        out_shape=jax.ShapeDtypeStruct((b * t, DA_HEADS * DA_V_DIM), F32),
        scratch_shapes=[pltpu.VMEM((LANES, 2 * blk), BF16),
                        pltpu.VMEM((blk, 2 * blk), F32),
                        pltpu.VMEM((blk, 2 * blk), F32),
                        pltpu.VMEM((1, 2 * blk), F32),
                        pltpu.VMEM((1, 2 * blk), F32),
                        pltpu.VMEM((LANES, 2 * blk), F32)],
        compiler_params=_params(("parallel", "parallel", "arbitrary")),
        name="dattn_prompt",
    )(qh, kh, vt, *lams)


def _ret_chunk(q, k, v, state, lg, c_len):
    cq, ck = q.shape[0], k.shape[0]
    i = lax.broadcasted_iota(jnp.int32, (cq, ck), 0)
    j = lax.broadcasted_iota(jnp.int32, (cq, ck), 1)
    diff = (i - j).astype(F32)
    decay = jnp.where(diff >= 0, jnp.exp(jnp.maximum(diff, 0.0) * lg), 0.0)
    qb, vb = q.astype(BF16), v.astype(BF16)
    inner = _dot_nt(qb, k.astype(BF16)) * decay
    o = _dot(inner.astype(BF16), vb)
    iq = lax.broadcasted_iota(jnp.int32, (cq, 1), 0).astype(F32)
    o = o + _dot(qb, state.astype(BF16)) * jnp.exp((iq + 1.0) * lg)
    ik = lax.broadcasted_iota(jnp.int32, (ck, 1), 0).astype(F32)
    k_decay = jnp.where(ik < c_len, jnp.exp((c_len - 1.0 - ik) * lg), 0.0)
    new_state = math.exp(c_len * lg) * state + _dot_tn((k * k_decay).astype(BF16), vb)
    return o, new_state


def _ret_prompt_kernel(q_ref, k_ref, v_ref, o_ref, s_ref):
    @pl.when(pl.program_id(0) == 0)
    def _():
        s_ref[...] = jnp.zeros(s_ref.shape, F32)

    for b in range(q_ref.shape[0]):
        for h in range(RET_HEADS):
            sl = slice(h * LANES, (h + 1) * LANES)
            o, s_new = _ret_chunk(q_ref[b, :, sl], k_ref[b, :, sl], v_ref[b, :, sl],
                                  s_ref[b, h], LOG_GAMMA[h], RET_CHUNK)
            o_ref[b, :, sl] = o
            s_ref[b, h] = s_new


def _ret_prompt(rq, rk, rv):
    b, t, w = rq.shape
    spec = pl.BlockSpec((b, RET_CHUNK, w), lambda c: (0, c, 0))
    return pl.pallas_call(
        _ret_prompt_kernel,
        grid=(t // RET_CHUNK,),
        in_specs=[spec] * 3,
        out_specs=[spec, pl.BlockSpec((b, RET_HEADS, RET_QK_DIM, RET_V_DIM),
                                      lambda c: (0, 0, 0, 0))],
        out_shape=[jax.ShapeDtypeStruct((b, t, w), F32),
                   jax.ShapeDtypeStruct((b, RET_HEADS, RET_QK_DIM, RET_V_DIM), F32)],
        compiler_params=_params(("arbitrary",)),
        name="ret_prompt",
    )(rq, rk, rv)


def _ret_sample_kernel(q_ref, k_ref, v_ref, s_ref, o_ref, sn_ref):
    c_len = q_ref.shape[1]
    pad = jnp.zeros((LANES - c_len, LANES), F32)
    for b in range(q_ref.shape[0]):
        for h in range(RET_HEADS):
            sl = slice(h * LANES, (h + 1) * LANES)
            k = jnp.concatenate([k_ref[b, :, sl], pad], axis=0)
            v = jnp.concatenate([v_ref[b, :, sl], pad], axis=0)
            o, s_new = _ret_chunk(q_ref[b, :, sl], k, v, s_ref[b, h], LOG_GAMMA[h], c_len)
            o_ref[b, :, sl] = o
            sn_ref[b, h] = s_new


def _ret_sample(rq, rk, rv, state):
    b, s, w = rq.shape
    rb = next(n for n in (4, 2, 1) if b % n == 0)
    spec = pl.BlockSpec((rb, s, w), lambda bi: (bi, 0, 0))
    sspec = pl.BlockSpec((rb, RET_HEADS, RET_QK_DIM, RET_V_DIM), lambda bi: (bi, 0, 0, 0))
    return pl.pallas_call(
        _ret_sample_kernel,
        grid=(b // rb,),
        in_specs=[spec] * 3 + [sspec],
        out_specs=[spec, sspec],
        out_shape=[jax.ShapeDtypeStruct((b, s, w), F32),
                   jax.ShapeDtypeStruct(state.shape, F32)],
        compiler_params=_params(("parallel",)),
        name="ret_sample",
    )(rq, rk, rv, state)


def _merge_kernel(da_ref, ro_ref, rg_ref, x_ref, gs_ref, gr_ref, w_ref, gc_ref, wq_ref,
                  y_ref, q_ref, *, lam_init):
    parts = []
    for h in range(DA_HEADS):
        sl = slice(h * LANES, (h + 1) * LANES)
        parts.append(_rms(da_ref[:, sl], gs_ref[...]) * (1.0 - lam_init))
    for h in range(RET_HEADS):
        sl = slice(h * LANES, (h + 1) * LANES)
        g = rg_ref[:, sl]
        parts.append(_rms(ro_ref[:, sl], gr_ref[...]) * (g * (1.0 / (1.0 + jnp.exp(-g)))))
    mix = jnp.concatenate(parts, axis=-1).astype(BF16)
    y = x_ref[...] + _dot(mix, w_ref[...])
    y_ref[...] = y
    hq = _rms(y, gc_ref[...]).astype(BF16)
    q_ref[...] = (_dot(hq, wq_ref[...]) * (CA_HEAD_DIM ** -0.5)).astype(BF16)


def _merge(da, ro, rg, x, g_sub, g_ret, w_bf16, g_cross, wq_bf16, lam_init, bm):
    m = x.shape[0]
    row = lambda i: (i, 0)
    spec = pl.BlockSpec((bm, D_MODEL), row)
    return pl.pallas_call(
        functools.partial(_merge_kernel, lam_init=lam_init),
        grid=(m // bm,),
        in_specs=[pl.BlockSpec((bm, GROUP_W), row)] * 3
                 + [spec, _const_spec((1, LANES)), _const_spec((1, LANES)),
                    _const_spec((D_MODEL, D_MODEL)), _const_spec((1, D_MODEL)),
                    _const_spec((D_MODEL, D_MODEL))],
        out_specs=[spec, spec],
        out_shape=[jax.ShapeDtypeStruct((m, D_MODEL), F32),
                   jax.ShapeDtypeStruct((m, D_MODEL), BF16)],
        compiler_params=_params(("parallel",)),
        name="merge_out",
    )(da, ro, rg, x, g_sub, g_ret, w_bf16, g_cross, wq_bf16)


def _mem_kv_kernel(x_ref, g_ref, wk_ref, wv_ref, k_ref, v_ref):
    h = _rms(x_ref[...], g_ref[...]).astype(BF16)
    k = _dot(h, wk_ref[...])
    v = _dot(h, wv_ref[...])
    for hd in range(CA_HEADS):
        sl = slice(hd * CA_HEAD_DIM, (hd + 1) * CA_HEAD_DIM)
        k_ref[:, hd, :] = k[:, sl]
        v_ref[:, hd, :] = v[:, sl]


def _mem_kv(x, g, wk_bf16, wv_bf16, bm):
    m = x.shape[0]
    spec = pl.BlockSpec((bm, D_MODEL), lambda i: (i, 0))
    ospec = pl.BlockSpec((bm, CA_HEADS, CA_HEAD_DIM), lambda i: (i, 0, 0))
    return pl.pallas_call(
        _mem_kv_kernel,
        grid=(m // bm,),
        in_specs=[spec, _const_spec((1, D_MODEL)),
                  _const_spec((D_MODEL, D_MODEL)), _const_spec((D_MODEL, D_MODEL))],
        out_specs=[ospec, ospec],
        out_shape=[jax.ShapeDtypeStruct((m, CA_HEADS, CA_HEAD_DIM), F32)] * 2,
        compiler_params=_params(("parallel",)),
        name="mem_kv",
    )(x, g, wk_bf16, wv_bf16)


def _cross_prompt_kernel(q_ref, mk_ref, mv_ref, o_ref, kh_ref, vh_ref):
    @pl.when(pl.program_id(1) == 0)
    def _():
        for hd in range(CA_HEADS):
            kh_ref[hd] = mk_ref[:, hd, :].astype(BF16)
            vh_ref[hd] = mv_ref[:, hd, :].astype(BF16)

    for hd in range(CA_HEADS):
        sl = slice(hd * CA_HEAD_DIM, (hd + 1) * CA_HEAD_DIM)
        s = _dot_nt(q_ref[0, :, sl], kh_ref[hd])
        p = jnp.exp(s - jnp.max(s, axis=-1, keepdims=True))
        a = p / jnp.sum(p, axis=-1, keepdims=True)
        o_ref[0, :, sl] = _dot(a.astype(BF16), vh_ref[hd]).astype(BF16)


def _cross_prompt(q, mk, mv, layer, bt):
    b, t, _ = q.shape
    n_mem = mk.shape[2]
    qspec = pl.BlockSpec((1, bt, D_MODEL), lambda bi, ti: (bi, ti, 0))
    mspec = pl.BlockSpec((None, None, n_mem, CA_HEADS, CA_HEAD_DIM),
                         lambda bi, ti: (layer, bi, 0, 0, 0))
    return pl.pallas_call(
        _cross_prompt_kernel,
        grid=(b, t // bt),
        in_specs=[qspec, mspec, mspec],
        out_specs=qspec,
        out_shape=jax.ShapeDtypeStruct(q.shape, BF16),
        scratch_shapes=[pltpu.VMEM((CA_HEADS, n_mem, CA_HEAD_DIM), BF16)] * 2,
        compiler_params=_params(("parallel", "arbitrary")),
        name="cross_prompt",
    )(q, mk, mv)


def _cross_sample_kernel(q_ref, mk_ref, mv_ref, o_ref):
    s_len = q_ref.shape[0]
    n_mem = mk_ref.shape[0]
    q = q_ref[...].astype(F32)
    q2 = jnp.concatenate([q[:, hd * CA_HEAD_DIM:(hd + 1) * CA_HEAD_DIM]
                          for hd in range(CA_HEADS)], axis=0).astype(BF16)
    k2 = mk_ref[...].reshape(n_mem * CA_HEADS, CA_HEAD_DIM).astype(BF16)
    v2 = mv_ref[...].reshape(n_mem * CA_HEADS, CA_HEAD_DIM).astype(BF16)
    s = _dot_nt(q2, k2)
    r = lax.broadcasted_iota(jnp.int32, s.shape, 0)
    c = lax.broadcasted_iota(jnp.int32, s.shape, 1)
    s = jnp.where((c & (CA_HEADS - 1)) == (r >> 3), s, -jnp.inf)
    p = jnp.exp(s - jnp.max(s, axis=-1, keepdims=True))
    a = p / jnp.sum(p, axis=-1, keepdims=True)
    o2 = _dot(a.astype(BF16), v2)
    o_ref[...] = jnp.concatenate([o2[hd * s_len:(hd + 1) * s_len]
                                  for hd in range(CA_HEADS)], axis=1).astype(BF16)


def _cross_sample(q, mk, mv, layer, s_len):
    assert s_len == 8 and CA_HEADS == 4
    n_mem = mk.shape[2]
    qspec = pl.BlockSpec((s_len, D_MODEL), lambda bi: (bi, 0))
    mspec = pl.BlockSpec((None, None, n_mem, CA_HEADS, CA_HEAD_DIM),
                         lambda bi: (layer, bi, 0, 0, 0))
    return pl.pallas_call(
        _cross_sample_kernel,
        grid=(q.shape[0] // s_len,),
        in_specs=[qspec, mspec, mspec],
        out_specs=qspec,
        out_shape=jax.ShapeDtypeStruct(q.shape, BF16),
        compiler_params=_params(("parallel",)),
        name="cross_sample",
    )(q, mk, mv)


def _mlp_kernel(y_ref, a_ref, wo_ref, g_ref, wu_ref, wd_ref, gf_ref, o_ref, *, ff_chunk):
    y = y_ref[...] + _dot(a_ref[...], wo_ref[...])
    h = _rms(y, g_ref[...]).astype(BF16)
    acc = y
    for c in range(D_FF // ff_chunk):
        sl = slice(c * ff_chunk, (c + 1) * ff_chunk)
        u = jnp.maximum(_dot(h, wu_ref[:, sl]), 0.0)
        acc = acc + _dot((u * u).astype(BF16), wd_ref[sl, :])
    o_ref[...] = _rms(acc, gf_ref[...])


def _mlp(y, attn, wo_bf16, g, wu_bf16, wd_bf16, g_final, bm):
    m = y.shape[0]
    row = lambda i: (i, 0)
    spec = pl.BlockSpec((bm, D_MODEL), row)
    return pl.pallas_call(
        functools.partial(_mlp_kernel, ff_chunk=1024),
        grid=(m // bm,),
        in_specs=[spec, spec, _const_spec((D_MODEL, D_MODEL)), _const_spec((1, D_MODEL)),
                  _const_spec((D_MODEL, D_FF)), _const_spec((D_FF, D_MODEL)),
                  _const_spec((1, D_MODEL))],
        out_specs=spec,
        out_shape=jax.ShapeDtypeStruct((m, D_MODEL), F32),
        compiler_params=_params(("parallel",)),
        name="mlp_final",
    )(y, attn, wo_bf16, g, wu_bf16, wd_bf16, g_final)


def _dattn_sample_kernel(pt_ref, q_ref, kn_ref, vn_ref, lq1, lk1, lq2, lk2, *rest,
                         pages_per_step, lam_init):
    del pt_ref
    k_refs = rest[:pages_per_step]
    v_refs = rest[pages_per_step:2 * pages_per_step]
    o_ref, qs_ref, m_ref, l_ref, acc_ref = rest[2 * pages_per_step:]
    j = pl.program_id(1)
    s_len = q_ref.shape[1]
    n_rows = 2 * DA_HEADS * s_len
    page_rows = k_refs[0].shape[0]

    def same_head(r, c):
        return (c & (DA_HEADS - 1)) == (r >> 4)

    @pl.when(j == 0)
    def _():
        parts = []
        lane = lax.broadcasted_iota(jnp.int32, (s_len, LANES), 1)
        for hd in range(DA_HEADS):
            q = q_ref[hd].astype(F32)
            parts += [jnp.where(lane < DA_HEAD_DIM, q, 0.0), jnp.where(lane >= DA_HEAD_DIM, q, 0.0)]
        qs = jnp.concatenate(parts, axis=0).astype(BF16)
        qs_ref[...] = qs
        pad = jnp.zeros((LANES - DA_HEADS * s_len, LANES), F32)
        kn = jnp.concatenate([kn_ref[...], pad], axis=0).astype(BF16)
        vn = jnp.concatenate([vn_ref[...], pad], axis=0).astype(BF16)
        s = _dot_nt(qs, kn)
        r = lax.broadcasted_iota(jnp.int32, s.shape, 0)
        c = lax.broadcasted_iota(jnp.int32, s.shape, 1)
        causal = (c >> 2) <= (r & (s_len - 1))
        s = jnp.where(same_head(r, c) & causal, s, -jnp.inf)
        m = jnp.max(s, axis=-1, keepdims=True)
        p = jnp.exp(s - m)
        m_ref[...] = m
        l_ref[...] = jnp.sum(p, axis=-1, keepdims=True)
        acc_ref[...] = _dot(p.astype(BF16), vn)

    def head_rows(refs, hd):
        return jnp.concatenate(
            [r[pl.ds(hd, page_rows // DA_HEADS, stride=DA_HEADS), :].astype(BF16) for r in refs],
            axis=0)

    hr = n_rows // DA_HEADS
    s = jnp.concatenate([_dot_nt(qs_ref[hd * hr:(hd + 1) * hr, :], head_rows(k_refs, hd))
                         for hd in range(DA_HEADS)], axis=0)
    m_old = m_ref[...]
    m_new = jnp.maximum(m_old, jnp.max(s, axis=-1, keepdims=True))
    alpha = jnp.exp(m_old - m_new)
    p = jnp.exp(s - m_new)
    l_ref[...] = alpha * l_ref[...] + jnp.sum(p, axis=-1, keepdims=True)
    pv = jnp.concatenate([_dot(p[hd * hr:(hd + 1) * hr].astype(BF16), head_rows(v_refs, hd))
                          for hd in range(DA_HEADS)], axis=0)
    acc_ref[...] = alpha * acc_ref[...] + pv
    m_ref[...] = m_new

    @pl.when(j == pl.num_programs(1) - 1)
    def _():
        lam = _diff_lambda(lq1, lk1, lq2, lk2, lam_init)
        o = acc_ref[...] / l_ref[...]
        for h in range(DA_HEADS):
            r0 = 2 * h * s_len
            o_ref[:, h * LANES:(h + 1) * LANES] = (o[r0:r0 + s_len]
                                                   - lam * o[r0 + s_len:r0 + 2 * s_len])


def _dattn_sample(qh, kf, vf, cache_k, cache_v, layer, page_table, lams, lam_init, pages_per_step):
    b, n_pages = page_table.shape
    s = qh.shape[1] // b
    assert s == 8 and DA_HEADS == 4
    depth, n_pool, page = cache_k.shape[:3]
    page_rows = page * DA_HEADS
    ck = cache_k.reshape(depth, n_pool, page_rows, LANES)
    cv = cache_v.reshape(depth, n_pool, page_rows, LANES)
    pt = page_table.reshape(-1)
    steps = n_pages // pages_per_step
    qspec = pl.BlockSpec((DA_HEADS, s, LANES), lambda bi, j, pt_r: (0, bi, 0))
    nspec = pl.BlockSpec((s * DA_HEADS, LANES), lambda bi, j, pt_r: (bi, 0))
    lspec = pl.BlockSpec((1, DA_HEAD_DIM), lambda *_: (0, 0))

    def page_spec(p):
        return pl.BlockSpec(
            (None, None, page_rows, LANES),
            lambda bi, j, pt_r: (layer, pt_r[bi * n_pages + j * pages_per_step + p], 0, 0))

    n_rows = 2 * DA_HEADS * s
    return pl.pallas_call(
        functools.partial(_dattn_sample_kernel, pages_per_step=pages_per_step,
                          lam_init=lam_init),
        grid_spec=pltpu.PrefetchScalarGridSpec(
            num_scalar_prefetch=1,
            grid=(b, steps),
            in_specs=[qspec, nspec, nspec] + [lspec] * 4
                     + [page_spec(p) for p in range(pages_per_step)] * 2,
            out_specs=pl.BlockSpec((s, DA_HEADS * LANES), lambda bi, j, pt_r: (bi, 0)),
            scratch_shapes=[pltpu.VMEM((n_rows, LANES), BF16),
                            pltpu.VMEM((n_rows, 1), F32),
                            pltpu.VMEM((n_rows, 1), F32),
                            pltpu.VMEM((n_rows, LANES), F32)]),
        out_shape=jax.ShapeDtypeStruct((b * s, DA_HEADS * LANES), F32),
        compiler_params=_params(("parallel", "arbitrary")),
        name="dattn_sample",
    )(pt, qh, kf, vf, *lams, *([ck] * pages_per_step), *([cv] * pages_per_step))


def kernel(x_prompt, x_sample, cache_k, cache_v, state_ret, cache_mem_k, cache_mem_v, page_table, mem_prompt, g_mix, w_in, lambda_q1, lambda_k1, lambda_q2, lambda_k2, g_diff_sub, g_ret, w_out, g_cross, g_mem, w_cq, w_ck, w_cv, w_co, g_mlp, w_up, w_down, g_final):
    bp, t, d = x_prompt.shape
    bd, s, _ = x_sample.shape
    depth = w_in.shape[0]
    assert depth == 1, "one pass over the final norm per layer stack of depth 1"
    n_mem = mem_prompt.shape[1]
    past = page_table.shape[1] * cache_k.shape[2]
    l = 0
    lam_init = 0.8 - 0.6 * math.exp(-0.3 * l)

    bm_p = 512
    tabs_p = _rope_tables(jnp.arange(t), DA_HEAD_DIM) + _rope_tables(jnp.arange(t), RET_QK_DIM)
    bm_s = min(256, bd * s)
    pos_s = past + jnp.arange(s)
    tabs_s = tuple(jnp.tile(tb, (bm_s // s, 1)) for tb in
                   _rope_tables(pos_s, DA_HEAD_DIM) + _rope_tables(pos_s, RET_QK_DIM))

    row2 = lambda a: a.reshape(1, -1)
    bf = lambda a: a.astype(BF16)
    w_in_b, w_out_b = bf(w_in[l]), bf(w_out[l])
    w_cq_b, w_ck_b, w_cv_b, w_co_b = bf(w_cq[l]), bf(w_ck[l]), bf(w_cv[l]), bf(w_co[l])
    w_up_b, w_down_b = bf(w_up[l]), bf(w_down[l])
    lams = (row2(lambda_q1[l]), row2(lambda_k1[l]), row2(lambda_q2[l]), row2(lambda_k2[l]))
    g_mix_l, g_cross_l, g_mlp_l = row2(g_mix[l]), row2(g_cross[l]), row2(g_mlp[l])
    g_sub_l, g_ret_l, g_mem_l, g_fin = row2(g_diff_sub[l]), row2(g_ret[l]), row2(g_mem[l]), row2(g_final)

    xp = x_prompt.reshape(bp * t, d)
    kf, vf, qh, kh, vt, rq, rk, rv, rg = _in_proj(xp, g_mix_l, w_in_b, tabs_p, bm_p, t // bm_p)
    seq = lambda a: a.reshape(bp, t, GROUP_W)
    da = _dattn_prompt(qh, kh, vt, lams, lam_init, bp, t)
    ro, s_p = _ret_prompt(seq(rq), seq(rk), seq(rv))
    y1, cq = _merge(da, ro.reshape(bp * t, GROUP_W), rg, xp,
                    g_sub_l, g_ret_l, w_out_b, g_cross_l, w_cq_b, lam_init, bm_p)
    mk, mv = _mem_kv(mem_prompt.reshape(bp * n_mem, d), g_mem_l, w_ck_b, w_cv_b, 256)
    mk = mk.reshape(1, bp, n_mem, CA_HEADS, CA_HEAD_DIM)
    mv = mv.reshape(1, bp, n_mem, CA_HEADS, CA_HEAD_DIM)
    ca = _cross_prompt(cq.reshape(bp, t, d), mk, mv, 0, 512).reshape(bp * t, d)
    y_prompt = _mlp(y1, ca, w_co_b, g_mlp_l, w_up_b, w_down_b, g_fin, bm_p).reshape(bp, t, d)

    xs = x_sample.reshape(bd * s, d)
    kfs, vfs, qh, _, _, rq, rk, rv, rg = _in_proj(xs, g_mix_l, w_in_b, tabs_s, bm_s, 1)
    req = lambda a: a.reshape(bd, s, GROUP_W)
    da = _dattn_sample(qh, kfs, vfs, cache_k, cache_v, l, page_table, lams, lam_init, 16)
    ro, s_s = _ret_sample(req(rq), req(rk), req(rv), state_ret[l])
    y1, cq = _merge(da, ro.reshape(bd * s, GROUP_W), rg, xs,
                    g_sub_l, g_ret_l, w_out_b, g_cross_l, w_cq_b, lam_init, bm_s)
    ca = _cross_sample(cq, cache_mem_k, cache_mem_v, l, s)
    y_sample = _mlp(y1, ca, w_co_b, g_mlp_l, w_up_b, w_down_b, g_fin, bm_s).reshape(bd, s, d)

    return (y_prompt, y_sample,
            kf.reshape(1, bp, t, DA_HEADS, 2 * DA_HEAD_DIM),
            vf.reshape(1, bp, t, DA_HEADS, DA_V_DIM),
            s_p[None], mk, mv,
            kfs.reshape(1, bd, s, DA_HEADS, 2 * DA_HEAD_DIM),
            vfs.reshape(1, bd, s, DA_HEADS, DA_V_DIM),
            s_s[None])
```

```python
import functools
import math

import jax
import jax.numpy as jnp
from jax import lax
from jax.experimental import pallas as pl
from jax.experimental.pallas import tpu as pltpu

F32 = jnp.float32
BF16 = jnp.bfloat16

D_MODEL = 1024
DA_HEADS = 4
DA_HEAD_DIM = 64
DA_V_DIM = 128
RET_HEADS = 4
RET_QK_DIM = 128
RET_V_DIM = 128
CA_HEADS = 4
CA_HEAD_DIM = 256
D_FF = 4096
GROUP_W = 512
N_GROUPS = 7
ROPE_THETA = 10000.0
RMS_EPS = 1e-6
RET_CHUNK = 128
LANES = 128
BF16_SUBLANES = 16
VT_ROWS = DA_V_DIM + BF16_SUBLANES
Q_SCALE = DA_HEAD_DIM ** -0.5 * math.log2(math.e)
VMEM_LIMIT = 56 * 1024 * 1024

LOG_GAMMA = tuple(math.log(1.0 - 2.0 ** (-5.0 - h)) for h in range(RET_HEADS))


def _dot(a, b):
    return jnp.dot(a, b, preferred_element_type=F32)


def _dot_nt(a, b):
    return lax.dot_general(a, b, (((1,), (1,)), ((), ())), preferred_element_type=F32)


def _dot_tn(a, b):
    return lax.dot_general(a, b, (((0,), (0,)), ((), ())), preferred_element_type=F32)


def _rms(x, g):
    ms = jnp.mean(x * x, axis=-1, keepdims=True)
    return x * lax.rsqrt(ms + RMS_EPS) * g


def _diff_lambda(lq1, lk1, lq2, lk2, lam_init):
    a = jnp.sum(lq1[...] * lk1[...], axis=-1, keepdims=True)
    b = jnp.sum(lq2[...] * lk2[...], axis=-1, keepdims=True)
    return jnp.exp(a) - jnp.exp(b) + lam_init


def _params(sem):
    return pltpu.CompilerParams(dimension_semantics=sem, vmem_limit_bytes=VMEM_LIMIT)


def _const_spec(shape):
    nd = len(shape)
    return pl.BlockSpec(shape, lambda *_: (0,) * nd, pipeline_mode=pl.Buffered(1))


def _in_proj_kernel(x_ref, g_ref, w_ref, cd_ref, sd_ref, cr_ref, sr_ref,
                    kf_ref, vf_ref, qh_ref, kh_ref, vt_ref, rq_ref, rk_ref, rv_ref, rg_ref):
    h = _rms(x_ref[...], g_ref[...]).astype(BF16)
    bm = h.shape[0]
    lane = lax.broadcasted_iota(jnp.int32, (bm, LANES), 1)
    first_half = (lane & (DA_HEAD_DIM - 1)) < (DA_HEAD_DIM // 2)
    cd, sd, cr, sr = cd_ref[...], sd_ref[...], cr_ref[...], sr_ref[...]

    def proj(g):
        return _dot(h, w_ref[:, g * GROUP_W:(g + 1) * GROUP_W])

    def rope_da(x):
        sw = jnp.where(first_half, pltpu.roll(x, LANES - 32, 1), pltpu.roll(x, 32, 1))
        return x * cd + sw * sd

    def rope_ret(x):
        return x * cr + pltpu.roll(x, 64, 1) * sr

    def store(ref, x, fn):
        for c in range(GROUP_W // LANES):
            sl = slice(c * LANES, (c + 1) * LANES)
            ref[:, sl] = fn(x[:, sl])

    dq, dk, dv = proj(0), proj(1), proj(2)
    for hd in range(DA_HEADS):
        sl = slice(hd * LANES, (hd + 1) * LANES)
        rows = pl.ds(hd, bm, stride=DA_HEADS)
        qh_ref[hd] = (rope_da(dq[:, sl]) * Q_SCALE).astype(BF16)
        k = rope_da(dk[:, sl])
        kf_ref[rows, :] = k
        kh_ref[hd] = k.astype(BF16)
        v = dv[:, sl]
        vf_ref[rows, :] = v
        vt_ref[hd, 0, :DA_V_DIM, :] = v.T.astype(BF16)
        vt_ref[hd, 0, DA_V_DIM:, :] = jnp.ones((VT_ROWS - DA_V_DIM, bm), BF16)
    store(rq_ref, proj(3), rope_ret)
    store(rk_ref, proj(4), lambda x: rope_ret(x) * (RET_QK_DIM ** -0.5))
    rv_ref[...] = proj(5)
    rg_ref[...] = proj(6)


def _in_proj(x, g, w_bf16, tabs, bm, tab_blocks):
    m = x.shape[0]
    row = lambda i: (i, 0)
    tab = lambda i: (i % tab_blocks, 0)
    wide = pl.BlockSpec((bm, GROUP_W), row)
    inter = pl.BlockSpec((bm * DA_HEADS, LANES), row)
    head = pl.BlockSpec((DA_HEADS, bm, LANES), lambda i: (0, i, 0))
    head_t = pl.BlockSpec((DA_HEADS, 1, VT_ROWS, bm), lambda i: (0, i, 0, 0))
    return pl.pallas_call(
        _in_proj_kernel,
        grid=(m // bm,),
        in_specs=[pl.BlockSpec((bm, D_MODEL), row),
                  _const_spec((1, D_MODEL)),
                  _const_spec((D_MODEL, N_GROUPS * GROUP_W))]
                 + [pl.BlockSpec((bm, LANES), tab)] * 4,
        out_specs=[inter, inter, head, head, head_t, wide, wide, wide, wide],
        out_shape=[jax.ShapeDtypeStruct((m * DA_HEADS, LANES), F32)] * 2
                  + [jax.ShapeDtypeStruct((DA_HEADS, m, LANES), BF16)] * 2
                  + [jax.ShapeDtypeStruct((DA_HEADS, m // bm, VT_ROWS, bm), BF16)]
                  + [jax.ShapeDtypeStruct((m, GROUP_W), F32)] * 4,
        compiler_params=_params(("parallel",)),
        name="in_proj",
    )(x, g, w_bf16, *tabs)


def _rope_tables(pos, dim):
    inv = 1.0 / (ROPE_THETA ** (jnp.arange(0, dim, 2, dtype=F32) / dim))
    ang = pos.astype(F32)[:, None] * inv[None, :]
    c, s = jnp.cos(ang), jnp.sin(ang)
    reps = LANES // dim
    return (jnp.tile(jnp.concatenate([c, c], axis=-1), (1, reps)),
            jnp.tile(jnp.concatenate([-s, s], axis=-1), (1, reps)))


def _dattn_prompt_kernel(q_ref, k_ref, vt_ref, lq1, lk1, lq2, lk2,
                         o_ref, qs_ref, sa_ref, sb_ref, m_ref, acc_ref, *, lam_init):
    qi = pl.program_id(2)
    bq = q_ref.shape[0]
    bk = vt_ref.shape[2]

    qt = q_ref[...].astype(F32).T
    row = lax.broadcasted_iota(jnp.int32, qt.shape, 0)
    qs_ref[...] = jnp.concatenate([jnp.where(row < DA_HEAD_DIM, qt, 0.0),
                                   jnp.where(row >= DA_HEAD_DIM, qt, 0.0)], axis=1).astype(BF16)
    m_ref[...] = jnp.full(m_ref.shape, -jnp.inf, F32)
    acc_ref[...] = jnp.zeros(acc_ref.shape, F32)

    def scores(kj):
        return _dot(k_ref[pl.ds(pl.multiple_of(kj * bk, bk), bk), :], qs_ref[...])

    def consume(s_ref, kj, diagonal):
        s = s_ref[...]
        if diagonal:
            kpos = lax.broadcasted_iota(jnp.int32, s.shape, 0)
            qpos = lax.broadcasted_iota(jnp.int32, s.shape, 1) & (bq - 1)
            s = jnp.where(kpos <= qpos, s, -jnp.inf)
        m_old = m_ref[...]
        m_new = jnp.maximum(m_old, jnp.max(s, axis=0, keepdims=True))
        alpha = jnp.exp2(m_old - m_new)
        p = jnp.exp2(s - m_new)
        acc_ref[...] = alpha * acc_ref[...] + _dot(vt_ref[kj], p.astype(BF16))
        m_ref[...] = m_new

    sa_ref[...] = scores(0)

    def pair(i, carry):
        kj = 2 * i
        sb_ref[...] = scores(kj + 1)
        consume(sa_ref, kj, False)
        sa_ref[...] = scores(kj + 2)
        consume(sb_ref, kj + 1, False)
        return carry

    lax.fori_loop(0, qi >> 1, pair, 0)

    @pl.when((qi & 1) == 1)
    def _():
        sb_ref[...] = scores(qi)
        consume(sa_ref, qi - 1, False)
        consume(sb_ref, qi, True)

    @pl.when((qi & 1) == 0)
    def _():
        consume(sa_ref, qi, True)

    lam = _diff_lambda(lq1, lk1, lq2, lk2, lam_init)
    o = acc_ref[:DA_V_DIM, :] / acc_ref[DA_V_DIM:DA_V_DIM + 1, :]
    o_ref[...] = (o[:, :bq] - lam * o[:, bq:]).T


def _dattn_prompt(qh, kh, vt, lams, lam_init, b, t):
    blk = vt.shape[3]
    nq = t // blk
    assert blk & (blk - 1) == 0
    lspec = pl.BlockSpec((1, DA_HEAD_DIM), lambda *_: (0, 0))
    return pl.pallas_call(
        functools.partial(_dattn_prompt_kernel, lam_init=lam_init),
        grid=(b, DA_HEADS, nq),
        in_specs=[pl.BlockSpec((None, blk, LANES), lambda bi, h, qi: (h, bi * nq + qi, 0)),
                  pl.BlockSpec((None, t, LANES), lambda bi, h, qi: (h, bi, 0)),
                  pl.BlockSpec((None, nq, VT_ROWS, blk), lambda bi, h, qi: (h, bi, 0, 0))]
                 + [lspec] * 4,
        out_specs=pl.BlockSpec((blk, LANES), lambda bi, h, qi: (bi * nq + qi, h)),
        out_shape=jax.ShapeDtypeStruct((b * t, DA_HEADS * DA_V_DIM), F32),
        scratch_shapes=[pltpu.VMEM((LANES, 2 * blk), BF16),
                        pltpu.VMEM((blk, 2 * blk), F32),
                        pltpu.VMEM((blk, 2 * blk), F32),
                        pltpu.VMEM((1, 2 * blk), F32),
                        pltpu.VMEM((VT_ROWS, 2 * blk), F32)],
        compiler_params=_params(("parallel", "parallel", "arbitrary")),
        name="dattn_prompt",
    )(qh, kh, vt, *lams)


def _ret_decays(cq, ck, lg, c_len):
    i = lax.broadcasted_iota(jnp.int32, (cq, ck), 0)
    j = lax.broadcasted_iota(jnp.int32, (cq, ck), 1)
    diff = (i - j).astype(F32)
    decay = jnp.where(diff >= 0, jnp.exp(jnp.maximum(diff, 0.0) * lg), 0.0)
    iq = lax.broadcasted_iota(jnp.int32, (cq, 1), 0).astype(F32)
    q_decay = jnp.exp((iq + 1.0) * lg)
    ik = lax.broadcasted_iota(jnp.int32, (ck, 1), 0).astype(F32)
    k_decay = jnp.where(ik < c_len, jnp.exp((c_len - 1.0 - ik) * lg), 0.0)
    return decay, q_decay, k_decay


def _ret_chunk(q, k, v, state, decays, state_decay):
    decay, q_decay, k_decay = decays
    qb, vb = q.astype(BF16), v.astype(BF16)
    inner = _dot_nt(qb, k.astype(BF16)) * decay
    o = _dot(inner.astype(BF16), vb) + _dot(qb, state.astype(BF16)) * q_decay
    new_state = state_decay * state + _dot_tn((k * k_decay).astype(BF16), vb)
    return o, new_state


def _ret_prompt_kernel(q_ref, k_ref, v_ref, o_ref, s_ref):
    @pl.when(pl.program_id(0) == 0)
    def _():
        s_ref[...] = jnp.zeros(s_ref.shape, F32)

    for h in range(RET_HEADS):
        sl = slice(h * LANES, (h + 1) * LANES)
        decays = _ret_decays(RET_CHUNK, RET_CHUNK, LOG_GAMMA[h], RET_CHUNK)
        state_decay = math.exp(RET_CHUNK * LOG_GAMMA[h])
        for b in range(q_ref.shape[0]):
            o, s_new = _ret_chunk(q_ref[b, :, sl], k_ref[b, :, sl], v_ref[b, :, sl],
                                  s_ref[b, h], decays, state_decay)
            o_ref[b, :, sl] = o
            s_ref[b, h] = s_new


def _ret_prompt(rq, rk, rv):
    b, t, w = rq.shape
    spec = pl.BlockSpec((b, RET_CHUNK, w), lambda c: (0, c, 0))
    return pl.pallas_call(
        _ret_prompt_kernel,
        grid=(t // RET_CHUNK,),
        in_specs=[spec] * 3,
        out_specs=[spec, pl.BlockSpec((b, RET_HEADS, RET_QK_DIM, RET_V_DIM),
                                      lambda c: (0, 0, 0, 0))],
        out_shape=[jax.ShapeDtypeStruct((b, t, w), F32),
                   jax.ShapeDtypeStruct((b, RET_HEADS, RET_QK_DIM, RET_V_DIM), F32)],
        compiler_params=_params(("arbitrary",)),
        name="ret_prompt",
    )(rq, rk, rv)


def _ret_sample_kernel(q_ref, k_ref, v_ref, s_ref, o_ref, sn_ref):
    c_len = q_ref.shape[1]
    pad = jnp.zeros((LANES - c_len, LANES), F32)
    for h in range(RET_HEADS):
        sl = slice(h * LANES, (h + 1) * LANES)
        decays = _ret_decays(c_len, LANES, LOG_GAMMA[h], c_len)
        state_decay = math.exp(c_len * LOG_GAMMA[h])
        for b in range(q_ref.shape[0]):
            k = jnp.concatenate([k_ref[b, :, sl], pad], axis=0)
            v = jnp.concatenate([v_ref[b, :, sl], pad], axis=0)
            o, s_new = _ret_chunk(q_ref[b, :, sl], k, v, s_ref[b, h], decays, state_decay)
            o_ref[b, :, sl] = o
            sn_ref[b, h] = s_new


def _ret_sample(rq, rk, rv, state):
    b, s, w = rq.shape
    rb = next(n for n in (4, 2, 1) if b % n == 0)
    spec = pl.BlockSpec((rb, s, w), lambda bi: (bi, 0, 0))
    sspec = pl.BlockSpec((rb, RET_HEADS, RET_QK_DIM, RET_V_DIM), lambda bi: (bi, 0, 0, 0))
    return pl.pallas_call(
        _ret_sample_kernel,
        grid=(b // rb,),
        in_specs=[spec] * 3 + [sspec],
        out_specs=[spec, sspec],
        out_shape=[jax.ShapeDtypeStruct((b, s, w), F32),
                   jax.ShapeDtypeStruct(state.shape, F32)],
        compiler_params=_params(("parallel",)),
        name="ret_sample",
    )(rq, rk, rv, state)


def _merge_kernel(da_ref, ro_ref, rg_ref, x_ref, gs_ref, gr_ref, w_ref, gc_ref, wq_ref,
                  y_ref, q_ref, *, lam_init):
    parts = []
    for h in range(DA_HEADS):
        sl = slice(h * LANES, (h + 1) * LANES)
        parts.append(_rms(da_ref[:, sl], gs_ref[...]) * (1.0 - lam_init))
    for h in range(RET_HEADS):
        sl = slice(h * LANES, (h + 1) * LANES)
        g = rg_ref[:, sl]
        parts.append(_rms(ro_ref[:, sl], gr_ref[...]) * (g * (1.0 / (1.0 + jnp.exp(-g)))))
    mix = jnp.concatenate(parts, axis=-1).astype(BF16)
    y = x_ref[...] + _dot(mix, w_ref[...])
    y_ref[...] = y
    hq = _rms(y, gc_ref[...]).astype(BF16)
    q_ref[...] = (_dot(hq, wq_ref[...]) * (CA_HEAD_DIM ** -0.5)).astype(BF16)


def _merge(da, ro, rg, x, g_sub, g_ret, w_bf16, g_cross, wq_bf16, lam_init, bm):
    m = x.shape[0]
    row = lambda i: (i, 0)
    spec = pl.BlockSpec((bm, D_MODEL), row)
    return pl.pallas_call(
        functools.partial(_merge_kernel, lam_init=lam_init),
        grid=(m // bm,),
        in_specs=[pl.BlockSpec((bm, GROUP_W), row)] * 3
                 + [spec, _const_spec((1, LANES)), _const_spec((1, LANES)),
                    _const_spec((D_MODEL, D_MODEL)), _const_spec((1, D_MODEL)),
                    _const_spec((D_MODEL, D_MODEL))],
        out_specs=[spec, spec],
        out_shape=[jax.ShapeDtypeStruct((m, D_MODEL), F32),
                   jax.ShapeDtypeStruct((m, D_MODEL), BF16)],
        compiler_params=_params(("parallel",)),
        name="merge_out",
    )(da, ro, rg, x, g_sub, g_ret, w_bf16, g_cross, wq_bf16)


def _mem_kv_kernel(x_ref, g_ref, wk_ref, wv_ref, k_ref, v_ref):
    h = _rms(x_ref[...], g_ref[...]).astype(BF16)
    k = _dot(h, wk_ref[...])
    v = _dot(h, wv_ref[...])
    for hd in range(CA_HEADS):
        sl = slice(hd * CA_HEAD_DIM, (hd + 1) * CA_HEAD_DIM)
        k_ref[:, hd, :] = k[:, sl]
        v_ref[:, hd, :] = v[:, sl]


def _mem_kv(x, g, wk_bf16, wv_bf16, bm):
    m = x.shape[0]
    spec = pl.BlockSpec((bm, D_MODEL), lambda i: (i, 0))
    ospec = pl.BlockSpec((bm, CA_HEADS, CA_HEAD_DIM), lambda i: (i, 0, 0))
    return pl.pallas_call(
        _mem_kv_kernel,
        grid=(m // bm,),
        in_specs=[spec, _const_spec((1, D_MODEL)),
                  _const_spec((D_MODEL, D_MODEL)), _const_spec((D_MODEL, D_MODEL))],
        out_specs=[ospec, ospec],
        out_shape=[jax.ShapeDtypeStruct((m, CA_HEADS, CA_HEAD_DIM), F32)] * 2,
        compiler_params=_params(("parallel",)),
        name="mem_kv",
    )(x, g, wk_bf16, wv_bf16)


def _cross_prompt_kernel(q_ref, mk_ref, mv_ref, o_ref, kh_ref, vh_ref):
    @pl.when(pl.program_id(1) == 0)
    def _():
        for hd in range(CA_HEADS):
            kh_ref[hd] = mk_ref[:, hd, :].astype(BF16)
            vh_ref[hd] = mv_ref[:, hd, :].astype(BF16)

    for hd in range(CA_HEADS):
        sl = slice(hd * CA_HEAD_DIM, (hd + 1) * CA_HEAD_DIM)
        s = _dot_nt(q_ref[0, :, sl], kh_ref[hd])
        p = jnp.exp(s - jnp.max(s, axis=-1, keepdims=True))
        a = p / jnp.sum(p, axis=-1, keepdims=True)
        o_ref[0, :, sl] = _dot(a.astype(BF16), vh_ref[hd]).astype(BF16)


def _cross_prompt(q, mk, mv, layer, bt):
    b, t, _ = q.shape
    n_mem = mk.shape[2]
    qspec = pl.BlockSpec((1, bt, D_MODEL), lambda bi, ti: (bi, ti, 0))
    mspec = pl.BlockSpec((None, None, n_mem, CA_HEADS, CA_HEAD_DIM),
                         lambda bi, ti: (layer, bi, 0, 0, 0))
    return pl.pallas_call(
        _cross_prompt_kernel,
        grid=(b, t // bt),
        in_specs=[qspec, mspec, mspec],
        out_specs=qspec,
        out_shape=jax.ShapeDtypeStruct(q.shape, BF16),
        scratch_shapes=[pltpu.VMEM((CA_HEADS, n_mem, CA_HEAD_DIM), BF16)] * 2,
        compiler_params=_params(("parallel", "arbitrary")),
        name="cross_prompt",
    )(q, mk, mv)


def _cross_sample_kernel(q_ref, mk_ref, mv_ref, o_ref):
    n_req, n_mem = mk_ref.shape[:2]
    s_len = q_ref.shape[0] // n_req
    q_all = q_ref[...].astype(F32)
    outs = []
    for b in range(n_req):
        q = q_all[b * s_len:(b + 1) * s_len]
        q2 = jnp.concatenate([q[:, hd * CA_HEAD_DIM:(hd + 1) * CA_HEAD_DIM]
                              for hd in range(CA_HEADS)], axis=0).astype(BF16)
        k2 = mk_ref[b].reshape(n_mem * CA_HEADS, CA_HEAD_DIM).astype(BF16)
        v2 = mv_ref[b].reshape(n_mem * CA_HEADS, CA_HEAD_DIM).astype(BF16)
        s = _dot_nt(q2, k2)
        r = lax.broadcasted_iota(jnp.int32, s.shape, 0)
        c = lax.broadcasted_iota(jnp.int32, s.shape, 1)
        s = jnp.where((c & (CA_HEADS - 1)) == (r >> 3), s, -jnp.inf)
        p = jnp.exp(s - jnp.max(s, axis=-1, keepdims=True))
        a = p / jnp.sum(p, axis=-1, keepdims=True)
        o2 = _dot(a.astype(BF16), v2)
        outs.append(jnp.concatenate([o2[hd * s_len:(hd + 1) * s_len]
                                     for hd in range(CA_HEADS)], axis=1))
    o_ref[...] = jnp.concatenate(outs, axis=0).astype(BF16)


def _cross_sample(q, mk, mv, layer, s_len):
    assert s_len == 8 and CA_HEADS == 4
    n_mem = mk.shape[2]
    b = q.shape[0] // s_len
    rb = next(n for n in (4, 2, 1) if b % n == 0)
    qspec = pl.BlockSpec((rb * s_len, D_MODEL), lambda bi: (bi, 0))
    mspec = pl.BlockSpec((None, rb, n_mem, CA_HEADS, CA_HEAD_DIM),
                         lambda bi: (layer, bi, 0, 0, 0))
    return pl.pallas_call(
        _cross_sample_kernel,
        grid=(b // rb,),
        in_specs=[qspec, mspec, mspec],
        out_specs=qspec,
        out_shape=jax.ShapeDtypeStruct(q.shape, BF16),
        compiler_params=_params(("parallel",)),
        name="cross_sample",
    )(q, mk, mv)


def _mlp_kernel(y_ref, a_ref, wo_ref, g_ref, wu_ref, wd_ref, gf_ref, o_ref, *, ff_chunk):
    y = y_ref[...] + _dot(a_ref[...], wo_ref[...])
    h = _rms(y, g_ref[...]).astype(BF16)
    acc = y
    for c in range(D_FF // ff_chunk):
        sl = slice(c * ff_chunk, (c + 1) * ff_chunk)
        u = jnp.maximum(_dot(h, wu_ref[:, sl]), 0.0)
        acc = acc + _dot((u * u).astype(BF16), wd_ref[sl, :])
    o_ref[...] = _rms(acc, gf_ref[...])


def _mlp(y, attn, wo_bf16, g, wu_bf16, wd_bf16, g_final, bm):
    m = y.shape[0]
    row = lambda i: (i, 0)
    spec = pl.BlockSpec((bm, D_MODEL), row)
    return pl.pallas_call(
        functools.partial(_mlp_kernel, ff_chunk=1024),
        grid=(m // bm,),
        in_specs=[spec, spec, _const_spec((D_MODEL, D_MODEL)), _const_spec((1, D_MODEL)),
                  _const_spec((D_MODEL, D_FF)), _const_spec((D_FF, D_MODEL)),
                  _const_spec((1, D_MODEL))],
        out_specs=spec,
        out_shape=jax.ShapeDtypeStruct((m, D_MODEL), F32),
        compiler_params=_params(("parallel",)),
        name="mlp_final",
    )(y, attn, wo_bf16, g, wu_bf16, wd_bf16, g_final)


def _dattn_sample_kernel(pt_ref, q_ref, kn_ref, vn_ref, lq1, lk1, lq2, lk2, *rest,
                         pages_per_step, lam_init):
    del pt_ref
    k_refs = rest[:pages_per_step]
    v_refs = rest[pages_per_step:2 * pages_per_step]
    o_ref, qs_ref, m_ref, l_ref, acc_ref = rest[2 * pages_per_step:]
    j = pl.program_id(1)
    s_len = q_ref.shape[1]
    n_rows = 2 * DA_HEADS * s_len
    page_rows = k_refs[0].shape[0]

    def same_head(r, c):
        return (c & (DA_HEADS - 1)) == (r >> 4)

    @pl.when(j == 0)
    def _():
        parts = []
        lane = lax.broadcasted_iota(jnp.int32, (s_len, LANES), 1)
        for hd in range(DA_HEADS):
            q = q_ref[hd].astype(F32)
            parts += [jnp.where(lane < DA_HEAD_DIM, q, 0.0), jnp.where(lane >= DA_HEAD_DIM, q, 0.0)]
        qs = jnp.concatenate(parts, axis=0).astype(BF16)
        qs_ref[...] = qs
        pad = jnp.zeros((LANES - DA_HEADS * s_len, LANES), F32)
        kn = jnp.concatenate([kn_ref[...], pad], axis=0).astype(BF16)
        vn = jnp.concatenate([vn_ref[...], pad], axis=0).astype(BF16)
        s = _dot_nt(qs, kn)
        r = lax.broadcasted_iota(jnp.int32, s.shape, 0)
        c = lax.broadcasted_iota(jnp.int32, s.shape, 1)
        causal = (c >> 2) <= (r & (s_len - 1))
        s = jnp.where(same_head(r, c) & causal, s, -jnp.inf)
        m = jnp.max(s, axis=-1, keepdims=True)
        p = jnp.exp2(s - m)
        m_ref[...] = m
        l_ref[...] = jnp.sum(p, axis=-1, keepdims=True)
        acc_ref[...] = _dot(p.astype(BF16), vn)

    def head_rows(refs, hd):
        return jnp.concatenate(
            [r[pl.ds(hd, page_rows // DA_HEADS, stride=DA_HEADS), :].astype(BF16) for r in refs],
            axis=0)

    hr = n_rows // DA_HEADS
    s = jnp.concatenate([_dot_nt(qs_ref[hd * hr:(hd + 1) * hr, :], head_rows(k_refs, hd))
                         for hd in range(DA_HEADS)], axis=0)
    m_old = m_ref[...]
    m_new = jnp.maximum(m_old, jnp.max(s, axis=-1, keepdims=True))
    alpha = jnp.exp2(m_old - m_new)
    p = jnp.exp2(s - m_new)
    l_ref[...] = alpha * l_ref[...] + jnp.sum(p, axis=-1, keepdims=True)
    pv = jnp.concatenate([_dot(p[hd * hr:(hd + 1) * hr].astype(BF16), head_rows(v_refs, hd))
                          for hd in range(DA_HEADS)], axis=0)
    acc_ref[...] = alpha * acc_ref[...] + pv
    m_ref[...] = m_new

    @pl.when(j == pl.num_programs(1) - 1)
    def _():
        lam = _diff_lambda(lq1, lk1, lq2, lk2, lam_init)
        o = acc_ref[...] / l_ref[...]
        for h in range(DA_HEADS):
            r0 = 2 * h * s_len
            o_ref[:, h * LANES:(h + 1) * LANES] = (o[r0:r0 + s_len]
                                                   - lam * o[r0 + s_len:r0 + 2 * s_len])


def _dattn_sample(qh, kf, vf, cache_k, cache_v, layer, page_table, lams, lam_init, pages_per_step):
    b, n_pages = page_table.shape
    s = qh.shape[1] // b
    assert s == 8 and DA_HEADS == 4
    depth, n_pool, page = cache_k.shape[:3]
    page_rows = page * DA_HEADS
    ck = cache_k.reshape(depth, n_pool, page_rows, LANES)
    cv = cache_v.reshape(depth, n_pool, page_rows, LANES)
    pt = page_table.reshape(-1)
    steps = n_pages // pages_per_step
    qspec = pl.BlockSpec((DA_HEADS, s, LANES), lambda bi, j, pt_r: (0, bi, 0))
    nspec = pl.BlockSpec((s * DA_HEADS, LANES), lambda bi, j, pt_r: (bi, 0))
    lspec = pl.BlockSpec((1, DA_HEAD_DIM), lambda *_: (0, 0))

    def page_spec(p):
        return pl.BlockSpec(
            (None, None, page_rows, LANES),
            lambda bi, j, pt_r: (layer, pt_r[bi * n_pages + j * pages_per_step + p], 0, 0))

    n_rows = 2 * DA_HEADS * s
    return pl.pallas_call(
        functools.partial(_dattn_sample_kernel, pages_per_step=pages_per_step,
                          lam_init=lam_init),
        grid_spec=pltpu.PrefetchScalarGridSpec(
            num_scalar_prefetch=1,
            grid=(b, steps),
            in_specs=[qspec, nspec, nspec] + [lspec] * 4
                     + [page_spec(p) for p in range(pages_per_step)] * 2,
            out_specs=pl.BlockSpec((s, DA_HEADS * LANES), lambda bi, j, pt_r: (bi, 0)),
            scratch_shapes=[pltpu.VMEM((n_rows, LANES), BF16),
                            pltpu.VMEM((n_rows, 1), F32),
                            pltpu.VMEM((n_rows, 1), F32),
                            pltpu.VMEM((n_rows, LANES), F32)]),
        out_shape=jax.ShapeDtypeStruct((b * s, DA_HEADS * LANES), F32),
        compiler_params=_params(("parallel", "arbitrary")),
        name="dattn_sample",
    )(pt, qh, kf, vf, *lams, *([ck] * pages_per_step), *([cv] * pages_per_step))


def kernel(x_prompt, x_sample, cache_k, cache_v, state_ret, cache_mem_k, cache_mem_v, page_table, mem_prompt, g_mix, w_in, lambda_q1, lambda_k1, lambda_q2, lambda_k2, g_diff_sub, g_ret, w_out, g_cross, g_mem, w_cq, w_ck, w_cv, w_co, g_mlp, w_up, w_down, g_final):
    bp, t, d = x_prompt.shape
    bd, s, _ = x_sample.shape
    depth = w_in.shape[0]
    assert depth == 1, "one pass over the final norm per layer stack of depth 1"
    n_mem = mem_prompt.shape[1]
    past = page_table.shape[1] * cache_k.shape[2]
    l = 0
    lam_init = 0.8 - 0.6 * math.exp(-0.3 * l)

    bm_p = 512
    tabs_p = _rope_tables(jnp.arange(t), DA_HEAD_DIM) + _rope_tables(jnp.arange(t), RET_QK_DIM)
    bm_s = min(256, bd * s)
    pos_s = past + jnp.arange(s)
    tabs_s = tuple(jnp.tile(tb, (bm_s // s, 1)) for tb in
                   _rope_tables(pos_s, DA_HEAD_DIM) + _rope_tables(pos_s, RET_QK_DIM))

    row2 = lambda a: a.reshape(1, -1)
    bf = lambda a: a.astype(BF16)
    w_in_b, w_out_b = bf(w_in[l]), bf(w_out[l])
    w_cq_b, w_ck_b, w_cv_b, w_co_b = bf(w_cq[l]), bf(w_ck[l]), bf(w_cv[l]), bf(w_co[l])
    w_up_b, w_down_b = bf(w_up[l]), bf(w_down[l])
    lams = (row2(lambda_q1[l]), row2(lambda_k1[l]), row2(lambda_q2[l]), row2(lambda_k2[l]))
    g_mix_l, g_cross_l, g_mlp_l = row2(g_mix[l]), row2(g_cross[l]), row2(g_mlp[l])
    g_sub_l, g_ret_l, g_mem_l, g_fin = row2(g_diff_sub[l]), row2(g_ret[l]), row2(g_mem[l]), row2(g_final)

    xp = x_prompt.reshape(bp * t, d)
    kf, vf, qh, kh, vt, rq, rk, rv, rg = _in_proj(xp, g_mix_l, w_in_b, tabs_p, bm_p, t // bm_p)
    seq = lambda a: a.reshape(bp, t, GROUP_W)
    da = _dattn_prompt(qh, kh, vt, lams, lam_init, bp, t)
    ro, s_p = _ret_prompt(seq(rq), seq(rk), seq(rv))
    y1, cq = _merge(da, ro.reshape(bp * t, GROUP_W), rg, xp,
                    g_sub_l, g_ret_l, w_out_b, g_cross_l, w_cq_b, lam_init, bm_p)
    mk, mv = _mem_kv(mem_prompt.reshape(bp * n_mem, d), g_mem_l, w_ck_b, w_cv_b, 256)
    mk = mk.reshape(1, bp, n_mem, CA_HEADS, CA_HEAD_DIM)
    mv = mv.reshape(1, bp, n_mem, CA_HEADS, CA_HEAD_DIM)
    ca = _cross_prompt(cq.reshape(bp, t, d), mk, mv, 0, 512).reshape(bp * t, d)
    y_prompt = _mlp(y1, ca, w_co_b, g_mlp_l, w_up_b, w_down_b, g_fin, bm_p).reshape(bp, t, d)

    xs = x_sample.reshape(bd * s, d)
    kfs, vfs, qh, _, _, rq, rk, rv, rg = _in_proj(xs, g_mix_l, w_in_b, tabs_s, bm_s, 1)
    req = lambda a: a.reshape(bd, s, GROUP_W)
    da = _dattn_sample(qh, kfs, vfs, cache_k, cache_v, l, page_table, lams, lam_init, 32)
    ro, s_s = _ret_sample(req(rq), req(rk), req(rv), state_ret[l])
    y1, cq = _merge(da, ro.reshape(bd * s, GROUP_W), rg, xs,
                    g_sub_l, g_ret_l, w_out_b, g_cross_l, w_cq_b, lam_init, bm_s)
    ca = _cross_sample(cq, cache_mem_k, cache_mem_v, l, s)
    y_sample = _mlp(y1, ca, w_co_b, g_mlp_l, w_up_b, w_down_b, g_fin, bm_s).reshape(bd, s, d)

    return (y_prompt, y_sample,
            kf.reshape(1, bp, t, DA_HEADS, 2 * DA_HEAD_DIM),
            vf.reshape(1, bp, t, DA_HEADS, DA_V_DIM),
            s_p[None], mk, mv,
            kfs.reshape(1, bd, s, DA_HEADS, 2 * DA_HEAD_DIM),
            vfs.reshape(1, bd, s, DA_HEADS, DA_V_DIM),
            s_s[None])
```

```python
import functools
import math

import jax
import jax.numpy as jnp
from jax import lax
from jax.experimental import pallas as pl
from jax.experimental.pallas import tpu as pltpu

F32 = jnp.float32
BF16 = jnp.bfloat16

D_MODEL = 1024
DA_HEADS = 4
DA_HEAD_DIM = 64
DA_V_DIM = 128
RET_HEADS = 4
RET_QK_DIM = 128
RET_V_DIM = 128
CA_HEADS = 4
CA_HEAD_DIM = 256
D_FF = 4096
GROUP_W = 512
N_GROUPS = 7
ROPE_THETA = 10000.0
RMS_EPS = 1e-6
RET_CHUNK = 128
LANES = 128
BF16_SUBLANES = 16
FF_CHUNK = 1024
VT_ROWS = DA_V_DIM + BF16_SUBLANES
Q_SCALE = DA_HEAD_DIM ** -0.5 * math.log2(math.e)
VMEM_LIMIT = 56 * 1024 * 1024

LOG_GAMMA = tuple(math.log(1.0 - 2.0 ** (-5.0 - h)) for h in range(RET_HEADS))


def _dot(a, b):
    return jnp.dot(a, b, preferred_element_type=F32)


def _dot_nt(a, b):
    return lax.dot_general(a, b, (((1,), (1,)), ((), ())), preferred_element_type=F32)


def _dot_tn(a, b):
    return lax.dot_general(a, b, (((0,), (0,)), ((), ())), preferred_element_type=F32)


def _rms(x, g):
    ms = jnp.mean(x * x, axis=-1, keepdims=True)
    return x * lax.rsqrt(ms + RMS_EPS) * g


def _diff_lambda(lq1, lk1, lq2, lk2, lam_init):
    a = jnp.sum(lq1[...] * lk1[...], axis=-1, keepdims=True)
    b = jnp.sum(lq2[...] * lk2[...], axis=-1, keepdims=True)
    return jnp.exp(a) - jnp.exp(b) + lam_init


def _params(sem):
    return pltpu.CompilerParams(dimension_semantics=sem, vmem_limit_bytes=VMEM_LIMIT)


def _const_spec(shape):
    nd = len(shape)
    return pl.BlockSpec(shape, lambda *_: (0,) * nd, pipeline_mode=pl.Buffered(1))


def _in_proj_kernel(x_ref, g_ref, w_ref, cd_ref, sd_ref, cr_ref, sr_ref,
                    kf_ref, vf_ref, qh_ref, kh_ref, vt_ref, rq_ref, rk_ref, rv_ref, rg_ref):
    h = _rms(x_ref[...], g_ref[...]).astype(BF16)
    bm = h.shape[0]
    lane = lax.broadcasted_iota(jnp.int32, (bm, LANES), 1)
    first_half = (lane & (DA_HEAD_DIM - 1)) < (DA_HEAD_DIM // 2)
    cd, sd, cr, sr = cd_ref[...], sd_ref[...], cr_ref[...], sr_ref[...]

    def proj(g):
        return _dot(h, w_ref[:, g * GROUP_W:(g + 1) * GROUP_W])

    def rope_da(x):
        sw = jnp.where(first_half, pltpu.roll(x, LANES - 32, 1), pltpu.roll(x, 32, 1))
        return x * cd + sw * sd

    def rope_ret(x):
        return x * cr + pltpu.roll(x, 64, 1) * sr

    def store(ref, x, fn):
        for c in range(GROUP_W // LANES):
            sl = slice(c * LANES, (c + 1) * LANES)
            ref[:, sl] = fn(x[:, sl])

    dq, dk, dv = proj(0), proj(1), proj(2)
    for hd in range(DA_HEADS):
        sl = slice(hd * LANES, (hd + 1) * LANES)
        rows = pl.ds(hd, bm, stride=DA_HEADS)
        qh_ref[hd] = (rope_da(dq[:, sl]) * Q_SCALE).astype(BF16)
        k = rope_da(dk[:, sl])
        kf_ref[rows, :] = k
        kh_ref[hd] = k.astype(BF16)
        v = dv[:, sl]
        vf_ref[rows, :] = v
        vt_ref[hd, 0, :DA_V_DIM, :] = v.T.astype(BF16)
        vt_ref[hd, 0, DA_V_DIM:, :] = jnp.ones((VT_ROWS - DA_V_DIM, bm), BF16)
    store(rq_ref, proj(3), rope_ret)
    store(rk_ref, proj(4), lambda x: rope_ret(x) * (RET_QK_DIM ** -0.5))
    rv_ref[...] = proj(5)
    rg_ref[...] = proj(6)


def _in_proj(x, g, w_bf16, tabs, bm, tab_blocks):
    m = x.shape[0]
    row = lambda i: (i, 0)
    tab = lambda i: (i % tab_blocks, 0)
    wide = pl.BlockSpec((bm, GROUP_W), row)
    inter = pl.BlockSpec((bm * DA_HEADS, LANES), row)
    head = pl.BlockSpec((DA_HEADS, bm, LANES), lambda i: (0, i, 0))
    head_t = pl.BlockSpec((DA_HEADS, 1, VT_ROWS, bm), lambda i: (0, i, 0, 0))
    return pl.pallas_call(
        _in_proj_kernel,
        grid=(m // bm,),
        in_specs=[pl.BlockSpec((bm, D_MODEL), row),
                  _const_spec((1, D_MODEL)),
                  _const_spec((D_MODEL, N_GROUPS * GROUP_W))]
                 + [pl.BlockSpec((bm, LANES), tab)] * 4,
        out_specs=[inter, inter, head, head, head_t, wide, wide, wide, wide],
        out_shape=[jax.ShapeDtypeStruct((m * DA_HEADS, LANES), F32)] * 2
                  + [jax.ShapeDtypeStruct((DA_HEADS, m, LANES), BF16)] * 2
                  + [jax.ShapeDtypeStruct((DA_HEADS, m // bm, VT_ROWS, bm), BF16)]
                  + [jax.ShapeDtypeStruct((m, GROUP_W), F32)] * 4,
        compiler_params=_params(("parallel",)),
        name="in_proj",
    )(x, g, w_bf16, *tabs)


def _rope_tables(pos, dim):
    inv = 1.0 / (ROPE_THETA ** (jnp.arange(0, dim, 2, dtype=F32) / dim))
    ang = pos.astype(F32)[:, None] * inv[None, :]
    c, s = jnp.cos(ang), jnp.sin(ang)
    reps = LANES // dim
    return (jnp.tile(jnp.concatenate([c, c], axis=-1), (1, reps)),
            jnp.tile(jnp.concatenate([-s, s], axis=-1), (1, reps)))


def _dattn_prompt_kernel(q_ref, k_ref, vt_ref, lq1, lk1, lq2, lk2,
                         o_ref, qs_ref, sa_ref, sb_ref, m_ref, acc_ref, *, lam_init):
    qi = pl.program_id(2)
    bq = q_ref.shape[0]
    bk = vt_ref.shape[2]

    qt = q_ref[...].astype(F32).T
    row = lax.broadcasted_iota(jnp.int32, qt.shape, 0)
    qs_ref[...] = jnp.concatenate([jnp.where(row < DA_HEAD_DIM, qt, 0.0),
                                   jnp.where(row >= DA_HEAD_DIM, qt, 0.0)], axis=1).astype(BF16)
    m_ref[...] = jnp.full(m_ref.shape, -jnp.inf, F32)
    acc_ref[...] = jnp.zeros(acc_ref.shape, F32)

    def scores(kj):
        return _dot(k_ref[pl.ds(pl.multiple_of(kj * bk, bk), bk), :], qs_ref[...])

    def consume(s_ref, kj, diagonal):
        s = s_ref[...]
        if diagonal:
            kpos = lax.broadcasted_iota(jnp.int32, s.shape, 0)
            qpos = lax.broadcasted_iota(jnp.int32, s.shape, 1) & (bq - 1)
            s = jnp.where(kpos <= qpos, s, -jnp.inf)
        m_old = m_ref[...]
        m_new = jnp.maximum(m_old, jnp.max(s, axis=0, keepdims=True))
        alpha = jnp.exp2(m_old - m_new)
        p = jnp.exp2(s - m_new)
        acc_ref[...] = alpha * acc_ref[...] + _dot(vt_ref[kj], p.astype(BF16))
        m_ref[...] = m_new

    sa_ref[...] = scores(0)

    def pair(i, carry):
        kj = 2 * i
        sb_ref[...] = scores(kj + 1)
        consume(sa_ref, kj, False)
        sa_ref[...] = scores(kj + 2)
        consume(sb_ref, kj + 1, False)
        return carry

    lax.fori_loop(0, qi >> 1, pair, 0)

    @pl.when((qi & 1) == 1)
    def _():
        sb_ref[...] = scores(qi)
        consume(sa_ref, qi - 1, False)
        consume(sb_ref, qi, True)

    @pl.when((qi & 1) == 0)
    def _():
        consume(sa_ref, qi, True)

    lam = _diff_lambda(lq1, lk1, lq2, lk2, lam_init)
    o = acc_ref[:DA_V_DIM, :] / acc_ref[DA_V_DIM:DA_V_DIM + 1, :]
    o_ref[...] = (o[:, :bq] - lam * o[:, bq:]).T


def _dattn_prompt(qh, kh, vt, lams, lam_init, b, t):
    blk = vt.shape[3]
    nq = t // blk
    assert blk & (blk - 1) == 0
    lspec = pl.BlockSpec((1, DA_HEAD_DIM), lambda *_: (0, 0))
    return pl.pallas_call(
        functools.partial(_dattn_prompt_kernel, lam_init=lam_init),
        grid=(b, DA_HEADS, nq),
        in_specs=[pl.BlockSpec((None, blk, LANES), lambda bi, h, qi: (h, bi * nq + qi, 0)),
                  pl.BlockSpec((None, t, LANES), lambda bi, h, qi: (h, bi, 0)),
                  pl.BlockSpec((None, nq, VT_ROWS, blk), lambda bi, h, qi: (h, bi, 0, 0))]
                 + [lspec] * 4,
        out_specs=pl.BlockSpec((blk, LANES), lambda bi, h, qi: (bi * nq + qi, h)),
        out_shape=jax.ShapeDtypeStruct((b * t, DA_HEADS * DA_V_DIM), F32),
        scratch_shapes=[pltpu.VMEM((LANES, 2 * blk), BF16),
                        pltpu.VMEM((blk, 2 * blk), F32),
                        pltpu.VMEM((blk, 2 * blk), F32),
                        pltpu.VMEM((1, 2 * blk), F32),
                        pltpu.VMEM((VT_ROWS, 2 * blk), F32)],
        compiler_params=_params(("parallel", "parallel", "arbitrary")),
        name="dattn_prompt",
    )(qh, kh, vt, *lams)


def _ret_decays(cq, ck, lg, c_len):
    i = lax.broadcasted_iota(jnp.int32, (cq, ck), 0)
    j = lax.broadcasted_iota(jnp.int32, (cq, ck), 1)
    diff = (i - j).astype(F32)
    decay = jnp.where(diff >= 0, jnp.exp(jnp.maximum(diff, 0.0) * lg), 0.0)
    iq = lax.broadcasted_iota(jnp.int32, (cq, 1), 0).astype(F32)
    q_decay = jnp.exp((iq + 1.0) * lg)
    ik = lax.broadcasted_iota(jnp.int32, (ck, 1), 0).astype(F32)
    k_decay = jnp.where(ik < c_len, jnp.exp((c_len - 1.0 - ik) * lg), 0.0)
    return decay, q_decay, k_decay


def _ret_chunk(q, k, v, state, decays, state_decay):
    decay, q_decay, k_decay = decays
    qb, vb = q.astype(BF16), v.astype(BF16)
    inner = _dot_nt(qb, k.astype(BF16)) * decay
    o = _dot(inner.astype(BF16), vb) + _dot(qb, state.astype(BF16)) * q_decay
    new_state = state_decay * state + _dot_tn((k * k_decay).astype(BF16), vb)
    return o, new_state


def _ret_prompt_kernel(q_ref, k_ref, v_ref, o_ref, s_ref):
    @pl.when(pl.program_id(0) == 0)
    def _():
        s_ref[...] = jnp.zeros(s_ref.shape, F32)

    for h in range(RET_HEADS):
        sl = slice(h * LANES, (h + 1) * LANES)
        decays = _ret_decays(RET_CHUNK, RET_CHUNK, LOG_GAMMA[h], RET_CHUNK)
        state_decay = math.exp(RET_CHUNK * LOG_GAMMA[h])
        for b in range(q_ref.shape[0]):
            o, s_new = _ret_chunk(q_ref[b, :, sl], k_ref[b, :, sl], v_ref[b, :, sl],
                                  s_ref[b, h], decays, state_decay)
            o_ref[b, :, sl] = o
            s_ref[b, h] = s_new


def _ret_prompt(rq, rk, rv):
    b, t, w = rq.shape
    spec = pl.BlockSpec((b, RET_CHUNK, w), lambda c: (0, c, 0))
    return pl.pallas_call(
        _ret_prompt_kernel,
        grid=(t // RET_CHUNK,),
        in_specs=[spec] * 3,
        out_specs=[spec, pl.BlockSpec((b, RET_HEADS, RET_QK_DIM, RET_V_DIM),
                                      lambda c: (0, 0, 0, 0))],
        out_shape=[jax.ShapeDtypeStruct((b, t, w), F32),
                   jax.ShapeDtypeStruct((b, RET_HEADS, RET_QK_DIM, RET_V_DIM), F32)],
        compiler_params=_params(("arbitrary",)),
        name="ret_prompt",
    )(rq, rk, rv)


def _ret_sample_kernel(q_ref, k_ref, v_ref, s_ref, o_ref, sn_ref):
    c_len = q_ref.shape[1]
    pad = jnp.zeros((LANES - c_len, LANES), F32)
    for h in range(RET_HEADS):
        sl = slice(h * LANES, (h + 1) * LANES)
        decays = _ret_decays(c_len, LANES, LOG_GAMMA[h], c_len)
        state_decay = math.exp(c_len * LOG_GAMMA[h])
        for b in range(q_ref.shape[0]):
            k = jnp.concatenate([k_ref[b, :, sl], pad], axis=0)
            v = jnp.concatenate([v_ref[b, :, sl], pad], axis=0)
            o, s_new = _ret_chunk(q_ref[b, :, sl], k, v, s_ref[b, h], decays, state_decay)
            o_ref[b, :, sl] = o
            sn_ref[b, h] = s_new


def _ret_sample(rq, rk, rv, state):
    b, s, w = rq.shape
    rb = next(n for n in (4, 2, 1) if b % n == 0)
    spec = pl.BlockSpec((rb, s, w), lambda bi: (bi, 0, 0))
    sspec = pl.BlockSpec((rb, RET_HEADS, RET_QK_DIM, RET_V_DIM), lambda bi: (bi, 0, 0, 0))
    return pl.pallas_call(
        _ret_sample_kernel,
        grid=(b // rb,),
        in_specs=[spec] * 3 + [sspec],
        out_specs=[spec, sspec],
        out_shape=[jax.ShapeDtypeStruct((b, s, w), F32),
                   jax.ShapeDtypeStruct(state.shape, F32)],
        compiler_params=_params(("parallel",)),
        name="ret_sample",
    )(rq, rk, rv, state)


def _merge_kernel(da_ref, ro_ref, rg_ref, x_ref, gs_ref, gr_ref, w_ref, gc_ref, wq_ref,
                  y_ref, q_ref, *, lam_init):
    parts = []
    for h in range(DA_HEADS):
        sl = slice(h * LANES, (h + 1) * LANES)
        parts.append(_rms(da_ref[:, sl], gs_ref[...]) * (1.0 - lam_init))
    for h in range(RET_HEADS):
        sl = slice(h * LANES, (h + 1) * LANES)
        g = rg_ref[:, sl]
        parts.append(_rms(ro_ref[:, sl], gr_ref[...]) * (g * (1.0 / (1.0 + jnp.exp(-g)))))
    mix = jnp.concatenate(parts, axis=-1).astype(BF16)
    y = x_ref[...] + _dot(mix, w_ref[...])
    y_ref[...] = y
    hq = _rms(y, gc_ref[...]).astype(BF16)
    q_ref[...] = (_dot(hq, wq_ref[...]) * (CA_HEAD_DIM ** -0.5)).astype(BF16)


def _merge(da, ro, rg, x, g_sub, g_ret, w_bf16, g_cross, wq_bf16, lam_init, bm):
    m = x.shape[0]
    row = lambda i: (i, 0)
    spec = pl.BlockSpec((bm, D_MODEL), row)
    return pl.pallas_call(
        functools.partial(_merge_kernel, lam_init=lam_init),
        grid=(m // bm,),
        in_specs=[pl.BlockSpec((bm, GROUP_W), row)] * 3
                 + [spec, _const_spec((1, LANES)), _const_spec((1, LANES)),
                    _const_spec((D_MODEL, D_MODEL)), _const_spec((1, D_MODEL)),
                    _const_spec((D_MODEL, D_MODEL))],
        out_specs=[spec, spec],
        out_shape=[jax.ShapeDtypeStruct((m, D_MODEL), F32),
                   jax.ShapeDtypeStruct((m, D_MODEL), BF16)],
        compiler_params=_params(("parallel",)),
        name="merge_out",
    )(da, ro, rg, x, g_sub, g_ret, w_bf16, g_cross, wq_bf16)


def _mem_kv_kernel(x_ref, g_ref, wk_ref, wv_ref, k_ref, v_ref):
    h = _rms(x_ref[...], g_ref[...]).astype(BF16)
    k = _dot(h, wk_ref[...])
    v = _dot(h, wv_ref[...])
    for hd in range(CA_HEADS):
        sl = slice(hd * CA_HEAD_DIM, (hd + 1) * CA_HEAD_DIM)
        k_ref[:, hd, :] = k[:, sl]
        v_ref[:, hd, :] = v[:, sl]


def _mem_kv(x, g, wk_bf16, wv_bf16, bm):
    m = x.shape[0]
    spec = pl.BlockSpec((bm, D_MODEL), lambda i: (i, 0))
    ospec = pl.BlockSpec((bm, CA_HEADS, CA_HEAD_DIM), lambda i: (i, 0, 0))
    return pl.pallas_call(
        _mem_kv_kernel,
        grid=(m // bm,),
        in_specs=[spec, _const_spec((1, D_MODEL)),
                  _const_spec((D_MODEL, D_MODEL)), _const_spec((D_MODEL, D_MODEL))],
        out_specs=[ospec, ospec],
        out_shape=[jax.ShapeDtypeStruct((m, CA_HEADS, CA_HEAD_DIM), F32)] * 2,
        compiler_params=_params(("parallel",)),
        name="mem_kv",
    )(x, g, wk_bf16, wv_bf16)


def _cross_prompt_kernel(q_ref, mk_ref, mv_ref, o_ref, kh_ref, vh_ref):
    @pl.when(pl.program_id(1) == 0)
    def _():
        for hd in range(CA_HEADS):
            kh_ref[hd] = mk_ref[:, hd, :].astype(BF16)
            vh_ref[hd] = mv_ref[:, hd, :].astype(BF16)

    for hd in range(CA_HEADS):
        sl = slice(hd * CA_HEAD_DIM, (hd + 1) * CA_HEAD_DIM)
        s = _dot_nt(q_ref[0, :, sl], kh_ref[hd])
        p = jnp.exp(s - jnp.max(s, axis=-1, keepdims=True))
        a = p / jnp.sum(p, axis=-1, keepdims=True)
        o_ref[0, :, sl] = _dot(a.astype(BF16), vh_ref[hd]).astype(BF16)


def _cross_prompt(q, mk, mv, layer, bt):
    b, t, _ = q.shape
    n_mem = mk.shape[2]
    qspec = pl.BlockSpec((1, bt, D_MODEL), lambda bi, ti: (bi, ti, 0))
    mspec = pl.BlockSpec((None, None, n_mem, CA_HEADS, CA_HEAD_DIM),
                         lambda bi, ti: (layer, bi, 0, 0, 0))
    return pl.pallas_call(
        _cross_prompt_kernel,
        grid=(b, t // bt),
        in_specs=[qspec, mspec, mspec],
        out_specs=qspec,
        out_shape=jax.ShapeDtypeStruct(q.shape, BF16),
        scratch_shapes=[pltpu.VMEM((CA_HEADS, n_mem, CA_HEAD_DIM), BF16)] * 2,
        compiler_params=_params(("parallel", "arbitrary")),
        name="cross_prompt",
    )(q, mk, mv)


def _cross_sample_kernel(q_ref, mk_ref, mv_ref, o_ref):
    n_req, n_mem = mk_ref.shape[:2]
    s_len = q_ref.shape[0] // n_req
    q_all = q_ref[...].astype(F32)
    outs = []
    for b in range(n_req):
        q = q_all[b * s_len:(b + 1) * s_len]
        q2 = jnp.concatenate([q[:, hd * CA_HEAD_DIM:(hd + 1) * CA_HEAD_DIM]
                              for hd in range(CA_HEADS)], axis=0).astype(BF16)
        k2 = mk_ref[b].reshape(n_mem * CA_HEADS, CA_HEAD_DIM).astype(BF16)
        v2 = mv_ref[b].reshape(n_mem * CA_HEADS, CA_HEAD_DIM).astype(BF16)
        s = _dot_nt(q2, k2)
        r = lax.broadcasted_iota(jnp.int32, s.shape, 0)
        c = lax.broadcasted_iota(jnp.int32, s.shape, 1)
        s = jnp.where((c & (CA_HEADS - 1)) == (r >> 3), s, -jnp.inf)
        p = jnp.exp(s - jnp.max(s, axis=-1, keepdims=True))
        a = p / jnp.sum(p, axis=-1, keepdims=True)
        o2 = _dot(a.astype(BF16), v2)
        outs.append(jnp.concatenate([o2[hd * s_len:(hd + 1) * s_len]
                                     for hd in range(CA_HEADS)], axis=1))
    o_ref[...] = jnp.concatenate(outs, axis=0).astype(BF16)


def _cross_sample(q, mk, mv, layer, s_len):
    assert s_len == 8 and CA_HEADS == 4
    n_mem = mk.shape[2]
    b = q.shape[0] // s_len
    rb = next(n for n in (4, 2, 1) if b % n == 0)
    qspec = pl.BlockSpec((rb * s_len, D_MODEL), lambda bi: (bi, 0))
    mspec = pl.BlockSpec((None, rb, n_mem, CA_HEADS, CA_HEAD_DIM),
                         lambda bi: (layer, bi, 0, 0, 0))
    return pl.pallas_call(
        _cross_sample_kernel,
        grid=(b // rb,),
        in_specs=[qspec, mspec, mspec],
        out_specs=qspec,
        out_shape=jax.ShapeDtypeStruct(q.shape, BF16),
        compiler_params=_params(("parallel",)),
        name="cross_sample",
    )(q, mk, mv)


def _mlp_kernel(y_ref, a_ref, wo_ref, g_ref, wu_ref, wd_ref, gf_ref, o_ref, *, ff_chunk):
    y = y_ref[...] + _dot(a_ref[...], wo_ref[...])
    h = _rms(y, g_ref[...]).astype(BF16)
    acc = y
    for c in range(D_FF // ff_chunk):
        sl = slice(c * ff_chunk, (c + 1) * ff_chunk)
        u = jnp.maximum(_dot(h, wu_ref[:, sl]), 0.0)
        acc = acc + _dot((u * u).astype(BF16), wd_ref[sl, :])
    o_ref[...] = _rms(acc, gf_ref[...])


def _mlp(y, attn, wo_bf16, g, wu_bf16, wd_bf16, g_final, bm):
    m = y.shape[0]
    row = lambda i: (i, 0)
    spec = pl.BlockSpec((bm, D_MODEL), row)
    return pl.pallas_call(
        functools.partial(_mlp_kernel, ff_chunk=1024),
        grid=(m // bm,),
        in_specs=[spec, spec, _const_spec((D_MODEL, D_MODEL)), _const_spec((1, D_MODEL)),
                  _const_spec((D_MODEL, D_FF)), _const_spec((D_FF, D_MODEL)),
                  _const_spec((1, D_MODEL))],
        out_specs=spec,
        out_shape=jax.ShapeDtypeStruct((m, D_MODEL), F32),
        compiler_params=_params(("parallel",)),
        name="mlp_final",
    )(y, attn, wo_bf16, g, wu_bf16, wd_bf16, g_final)


def _dattn_sample_kernel(pt_ref, q_ref, kn_ref, vn_ref, lq1, lk1, lq2, lk2, *rest,
                         pages_per_step, lam_init, n_slices):
    del pt_ref
    k_refs = rest[:pages_per_step]
    v_refs = rest[pages_per_step:2 * pages_per_step]
    (y_ref, a_ref, wo_ref, g_ref, wu_ref, wd_ref, gf_ref, o_ref, yo_ref,
     qs_ref, m_ref, l_ref, acc_ref, h_ref, mlp_ref) = rest[2 * pages_per_step:]
    j = pl.program_id(1)
    s_len = q_ref.shape[1]
    n_rows = 2 * DA_HEADS * s_len
    page_rows = k_refs[0].shape[0]

    def same_head(r, c):
        return (c & (DA_HEADS - 1)) == (r >> 4)

    @pl.when(j == 0)
    def _():
        parts = []
        lane = lax.broadcasted_iota(jnp.int32, (s_len, LANES), 1)
        for hd in range(DA_HEADS):
            q = q_ref[hd].astype(F32)
            parts += [jnp.where(lane < DA_HEAD_DIM, q, 0.0), jnp.where(lane >= DA_HEAD_DIM, q, 0.0)]
        qs = jnp.concatenate(parts, axis=0).astype(BF16)
        qs_ref[...] = qs
        pad = jnp.zeros((LANES - DA_HEADS * s_len, LANES), F32)
        kn = jnp.concatenate([kn_ref[...], pad], axis=0).astype(BF16)
        vn = jnp.concatenate([vn_ref[...], pad], axis=0).astype(BF16)
        s = _dot_nt(qs, kn)
        r = lax.broadcasted_iota(jnp.int32, s.shape, 0)
        c = lax.broadcasted_iota(jnp.int32, s.shape, 1)
        causal = (c >> 2) <= (r & (s_len - 1))
        s = jnp.where(same_head(r, c) & causal, s, -jnp.inf)
        m = jnp.max(s, axis=-1, keepdims=True)
        p = jnp.exp2(s - m)
        m_ref[...] = m
        l_ref[...] = jnp.sum(p, axis=-1, keepdims=True)
        acc_ref[...] = _dot(p.astype(BF16), vn)

    def head_rows(refs, hd):
        return jnp.concatenate(
            [r[pl.ds(hd, page_rows // DA_HEADS, stride=DA_HEADS), :].astype(BF16) for r in refs],
            axis=0)

    hr = n_rows // DA_HEADS
    s = jnp.concatenate([_dot_nt(qs_ref[hd * hr:(hd + 1) * hr, :], head_rows(k_refs, hd))
                         for hd in range(DA_HEADS)], axis=0)
    m_old = m_ref[...]
    m_new = jnp.maximum(m_old, jnp.max(s, axis=-1, keepdims=True))
    alpha = jnp.exp2(m_old - m_new)
    p = jnp.exp2(s - m_new)
    l_ref[...] = alpha * l_ref[...] + jnp.sum(p, axis=-1, keepdims=True)
    pv = jnp.concatenate([_dot(p[hd * hr:(hd + 1) * hr].astype(BF16), head_rows(v_refs, hd))
                          for hd in range(DA_HEADS)], axis=0)
    acc_ref[...] = alpha * acc_ref[...] + pv
    m_ref[...] = m_new

    @pl.when(j == pl.num_programs(1) - 1)
    def _():
        lam = _diff_lambda(lq1, lk1, lq2, lk2, lam_init)
        o = acc_ref[...] / l_ref[...]
        for h in range(DA_HEADS):
            r0 = 2 * h * s_len
            o_ref[:, h * LANES:(h + 1) * LANES] = (o[r0:r0 + s_len]
                                                   - lam * o[r0 + s_len:r0 + 2 * s_len])

    n_chunks = wu_ref.shape[0]
    n_groups = n_slices // n_chunks
    group_rows = y_ref.shape[0] // n_groups
    sidx = (pl.program_id(0) * pl.num_programs(1) + j) % n_slices
    c = sidx % n_chunks
    rows = pl.ds(pl.multiple_of((sidx // n_chunks) * group_rows, group_rows), group_rows)

    @pl.when(c == 0)
    def _():
        y = y_ref[rows, :] + _dot(a_ref[rows, :], wo_ref[...])
        mlp_ref[rows, :] = y
        h_ref[rows, :] = _rms(y, g_ref[...]).astype(BF16)

    u = jnp.maximum(_dot(h_ref[rows, :], wu_ref[c]), 0.0)
    mlp_ref[rows, :] += _dot((u * u).astype(BF16), wd_ref[c])

    @pl.when(c == n_chunks - 1)
    def _():
        yo_ref[rows, :] = _rms(mlp_ref[rows, :], gf_ref[...])


def _dattn_sample_mlp(qh, kf, vf, cache_k, cache_v, layer, page_table, lams, lam_init,
                      pages_per_step, y, attn, wo_bf16, g, wu_bf16, wd_bf16, g_final, bm):
    b, n_pages = page_table.shape
    s = qh.shape[1] // b
    assert s == 8 and DA_HEADS == 4
    depth, n_pool, page = cache_k.shape[:3]
    page_rows = page * DA_HEADS
    ck = cache_k.reshape(depth, n_pool, page_rows, LANES)
    cv = cache_v.reshape(depth, n_pool, page_rows, LANES)
    pt = page_table.reshape(-1)
    steps = n_pages // pages_per_step
    m = y.shape[0]
    n_slices, rem = divmod(b * steps * bm, m)
    n_chunks = D_FF // FF_CHUNK
    assert rem == 0 and n_slices % n_chunks == 0, (b, steps, bm, m)
    assert bm % (BF16_SUBLANES * (n_slices // n_chunks)) == 0
    ff = FF_CHUNK
    wu3 = wu_bf16.reshape(D_MODEL, n_chunks, ff).transpose(1, 0, 2)
    wd3 = wd_bf16.reshape(n_chunks, ff, D_MODEL)

    qspec = pl.BlockSpec((DA_HEADS, s, LANES), lambda bi, j, pt_r: (0, bi, 0))
    nspec = pl.BlockSpec((s * DA_HEADS, LANES), lambda bi, j, pt_r: (bi, 0))
    lspec = pl.BlockSpec((1, DA_HEAD_DIM), lambda *_: (0, 0))
    mspec = pl.BlockSpec((bm, D_MODEL), lambda bi, j, pt_r: ((bi * steps + j) // n_slices, 0))

    def page_spec(p):
        return pl.BlockSpec(
            (None, None, page_rows, LANES),
            lambda bi, j, pt_r: (layer, pt_r[bi * n_pages + j * pages_per_step + p], 0, 0))

    n_rows = 2 * DA_HEADS * s
    return pl.pallas_call(
        functools.partial(_dattn_sample_kernel, pages_per_step=pages_per_step,
                          lam_init=lam_init, n_slices=n_slices),
        grid_spec=pltpu.PrefetchScalarGridSpec(
            num_scalar_prefetch=1,
            grid=(b, steps),
            in_specs=[qspec, nspec, nspec] + [lspec] * 4
                     + [page_spec(p) for p in range(pages_per_step)] * 2
                     + [mspec, mspec, _const_spec((D_MODEL, D_MODEL)), _const_spec((1, D_MODEL)),
                        _const_spec((n_chunks, D_MODEL, ff)), _const_spec((n_chunks, ff, D_MODEL)),
                        _const_spec((1, D_MODEL))],
            out_specs=[pl.BlockSpec((s, DA_HEADS * LANES), lambda bi, j, pt_r: (bi, 0)), mspec],
            scratch_shapes=[pltpu.VMEM((n_rows, LANES), BF16),
                            pltpu.VMEM((n_rows, 1), F32),
                            pltpu.VMEM((n_rows, 1), F32),
                            pltpu.VMEM((n_rows, LANES), F32),
                            pltpu.VMEM((bm, D_MODEL), BF16),
                            pltpu.VMEM((bm, D_MODEL), F32)]),
        out_shape=[jax.ShapeDtypeStruct((b * s, DA_HEADS * LANES), F32),
                   jax.ShapeDtypeStruct((m, D_MODEL), F32)],
        compiler_params=_params(("arbitrary", "arbitrary")),
        name="dattn_sample_mlp",
    )(pt, qh, kf, vf, *lams, *([ck] * pages_per_step), *([cv] * pages_per_step),
      y, attn, wo_bf16, g, wu3, wd3, g_final)


def kernel(x_prompt, x_sample, cache_k, cache_v, state_ret, cache_mem_k, cache_mem_v, page_table, mem_prompt, g_mix, w_in, lambda_q1, lambda_k1, lambda_q2, lambda_k2, g_diff_sub, g_ret, w_out, g_cross, g_mem, w_cq, w_ck, w_cv, w_co, g_mlp, w_up, w_down, g_final):
    bp, t, d = x_prompt.shape
    bd, s, _ = x_sample.shape
    depth = w_in.shape[0]
    assert depth == 1, "one pass over the final norm per layer stack of depth 1"
    n_mem = mem_prompt.shape[1]
    past = page_table.shape[1] * cache_k.shape[2]
    l = 0
    lam_init = 0.8 - 0.6 * math.exp(-0.3 * l)

    bm_p = 512
    tabs_p = _rope_tables(jnp.arange(t), DA_HEAD_DIM) + _rope_tables(jnp.arange(t), RET_QK_DIM)
    bm_s = min(256, bd * s)
    pos_s = past + jnp.arange(s)
    tabs_s = tuple(jnp.tile(tb, (bm_s // s, 1)) for tb in
                   _rope_tables(pos_s, DA_HEAD_DIM) + _rope_tables(pos_s, RET_QK_DIM))

    row2 = lambda a: a.reshape(1, -1)
    bf = lambda a: a.astype(BF16)
    w_in_b, w_out_b = bf(w_in[l]), bf(w_out[l])
    w_cq_b, w_ck_b, w_cv_b, w_co_b = bf(w_cq[l]), bf(w_ck[l]), bf(w_cv[l]), bf(w_co[l])
    w_up_b, w_down_b = bf(w_up[l]), bf(w_down[l])
    lams = (row2(lambda_q1[l]), row2(lambda_k1[l]), row2(lambda_q2[l]), row2(lambda_k2[l]))
    g_mix_l, g_cross_l, g_mlp_l = row2(g_mix[l]), row2(g_cross[l]), row2(g_mlp[l])
    g_sub_l, g_ret_l, g_mem_l, g_fin = row2(g_diff_sub[l]), row2(g_ret[l]), row2(g_mem[l]), row2(g_final)

    xp = x_prompt.reshape(bp * t, d)
    kf, vf, qh, kh, vt, rq, rk, rv, rg = _in_proj(xp, g_mix_l, w_in_b, tabs_p, bm_p, t // bm_p)
    seq = lambda a: a.reshape(bp, t, GROUP_W)
    da = _dattn_prompt(qh, kh, vt, lams, lam_init, bp, t)
    ro, s_p = _ret_prompt(seq(rq), seq(rk), seq(rv))
    y1, cq = _merge(da, ro.reshape(bp * t, GROUP_W), rg, xp,
                    g_sub_l, g_ret_l, w_out_b, g_cross_l, w_cq_b, lam_init, bm_p)
    mk, mv = _mem_kv(mem_prompt.reshape(bp * n_mem, d), g_mem_l, w_ck_b, w_cv_b, 256)
    mk = mk.reshape(1, bp, n_mem, CA_HEADS, CA_HEAD_DIM)
    mv = mv.reshape(1, bp, n_mem, CA_HEADS, CA_HEAD_DIM)
    ca_p = _cross_prompt(cq.reshape(bp, t, d), mk, mv, 0, 512).reshape(bp * t, d)
    y1_p = y1

    xs = x_sample.reshape(bd * s, d)
    kfs, vfs, qh, _, _, rq, rk, rv, rg = _in_proj(xs, g_mix_l, w_in_b, tabs_s, bm_s, 1)
    req = lambda a: a.reshape(bd, s, GROUP_W)
    da, y_prompt = _dattn_sample_mlp(qh, kfs, vfs, cache_k, cache_v, l, page_table, lams, lam_init,
                                     16, y1_p, ca_p, w_co_b, g_mlp_l, w_up_b, w_down_b, g_fin, bm_p)
    y_prompt = y_prompt.reshape(bp, t, d)
    ro, s_s = _ret_sample(req(rq), req(rk), req(rv), state_ret[l])
    y1, cq = _merge(da, ro.reshape(bd * s, GROUP_W), rg, xs,
                    g_sub_l, g_ret_l, w_out_b, g_cross_l, w_cq_b, lam_init, bm_s)
    ca = _cross_sample(cq, cache_mem_k, cache_mem_v, l, s)
    y_sample = _mlp(y1, ca, w_co_b, g_mlp_l, w_up_b, w_down_b, g_fin, bm_s).reshape(bd, s, d)

    return (y_prompt, y_sample,
            kf.reshape(1, bp, t, DA_HEADS, 2 * DA_HEAD_DIM),
            vf.reshape(1, bp, t, DA_HEADS, DA_V_DIM),
            s_p[None], mk, mv,
            kfs.reshape(1, bd, s, DA_HEADS, 2 * DA_HEAD_DIM),
            vfs.reshape(1, bd, s, DA_HEADS, DA_V_DIM),
            s_s[None])
```

```python
import functools
import math

import jax
import jax.numpy as jnp
import numpy as np
from jax import lax
from jax.experimental import pallas as pl
from jax.experimental.pallas import tpu as pltpu

F32 = jnp.float32
BF16 = jnp.bfloat16

D_MODEL = 1024
DA_HEADS = 4
DA_HEAD_DIM = 64
DA_V_DIM = 128
RET_HEADS = 4
RET_QK_DIM = 128
RET_V_DIM = 128
CA_HEADS = 4
CA_HEAD_DIM = 256
D_FF = 4096
GROUP_W = 512
N_GROUPS = 7
ROPE_THETA = 10000.0
RMS_EPS = 1e-6
RET_CHUNK = 128
LANES = 128
BF16_SUBLANES = 16
VT_ROWS = DA_V_DIM + BF16_SUBLANES
Q_SCALE = DA_HEAD_DIM ** -0.5 * math.log2(math.e)
VMEM_LIMIT = 56 * 1024 * 1024

LOG_GAMMA = tuple(math.log(1.0 - 2.0 ** (-5.0 - h)) for h in range(RET_HEADS))


def _dot(a, b):
    return jnp.dot(a, b, preferred_element_type=F32)


def _dot_nt(a, b):
    return lax.dot_general(a, b, (((1,), (1,)), ((), ())), preferred_element_type=F32)


def _dot_tn(a, b):
    return lax.dot_general(a, b, (((0,), (0,)), ((), ())), preferred_element_type=F32)


def _rms(x, g):
    ms = jnp.mean(x * x, axis=-1, keepdims=True)
    return x * lax.rsqrt(ms + RMS_EPS) * g


def _diff_lambda(lq1, lk1, lq2, lk2, lam_init):
    a = jnp.sum(lq1[...] * lk1[...], axis=-1, keepdims=True)
    b = jnp.sum(lq2[...] * lk2[...], axis=-1, keepdims=True)
    return jnp.exp(a) - jnp.exp(b) + lam_init


def _params(sem):
    return pltpu.CompilerParams(dimension_semantics=sem, vmem_limit_bytes=VMEM_LIMIT)


def _const_spec(shape):
    nd = len(shape)
    return pl.BlockSpec(shape, lambda *_: (0,) * nd, pipeline_mode=pl.Buffered(1))


def _in_proj_kernel(x_ref, g_ref, w_ref, cd_ref, sd_ref, cr_ref, sr_ref,
                    kf_ref, vf_ref, qh_ref, kh_ref, vt_ref, rq_ref, rk_ref, rv_ref, rg_ref):
    h = _rms(x_ref[...], g_ref[...]).astype(BF16)
    bm = h.shape[0]
    lane = lax.broadcasted_iota(jnp.int32, (bm, LANES), 1)
    first_half = (lane & (DA_HEAD_DIM - 1)) < (DA_HEAD_DIM // 2)
    cd, sd, cr, sr = cd_ref[...], sd_ref[...], cr_ref[...], sr_ref[...]

    def proj(g):
        return _dot(h, w_ref[:, g * GROUP_W:(g + 1) * GROUP_W])

    def rope_da(x):
        sw = jnp.where(first_half, pltpu.roll(x, LANES - 32, 1), pltpu.roll(x, 32, 1))
        return x * cd + sw * sd

    def rope_ret(x):
        return x * cr + pltpu.roll(x, 64, 1) * sr

    def store(ref, x, fn):
        for c in range(GROUP_W // LANES):
            sl = slice(c * LANES, (c + 1) * LANES)
            ref[:, sl] = fn(x[:, sl])

    dq, dk, dv = proj(0), proj(1), proj(2)
    for hd in range(DA_HEADS):
        sl = slice(hd * LANES, (hd + 1) * LANES)
        rows = pl.ds(hd, bm, stride=DA_HEADS)
        qh_ref[hd] = (rope_da(dq[:, sl]) * Q_SCALE).astype(BF16)
        k = rope_da(dk[:, sl])
        kf_ref[rows, :] = k
        kh_ref[hd] = k.astype(BF16)
        v = dv[:, sl]
        vf_ref[rows, :] = v
        vt_ref[hd, 0, :DA_V_DIM, :] = v.T.astype(BF16)
        vt_ref[hd, 0, DA_V_DIM:, :] = jnp.ones((VT_ROWS - DA_V_DIM, bm), BF16)
    store(rq_ref, proj(3), rope_ret)
    store(rk_ref, proj(4), lambda x: rope_ret(x) * (RET_QK_DIM ** -0.5))
    rv_ref[...] = proj(5)
    rg_ref[...] = proj(6)


def _in_proj(x, g, w_bf16, tabs, bm, tab_blocks):
    m = x.shape[0]
    row = lambda i: (i, 0)
    tab = lambda i: (i % tab_blocks, 0)
    wide = pl.BlockSpec((bm, GROUP_W), row)
    inter = pl.BlockSpec((bm * DA_HEADS, LANES), row)
    head = pl.BlockSpec((DA_HEADS, bm, LANES), lambda i: (0, i, 0))
    head_t = pl.BlockSpec((DA_HEADS, 1, VT_ROWS, bm), lambda i: (0, i, 0, 0))
    return pl.pallas_call(
        _in_proj_kernel,
        grid=(m // bm,),
        in_specs=[pl.BlockSpec((bm, D_MODEL), row),
                  _const_spec((1, D_MODEL)),
                  _const_spec((D_MODEL, N_GROUPS * GROUP_W))]
                 + [pl.BlockSpec((bm, LANES), tab)] * 4,
        out_specs=[inter, inter, head, head, head_t, wide, wide, wide, wide],
        out_shape=[jax.ShapeDtypeStruct((m * DA_HEADS, LANES), F32)] * 2
                  + [jax.ShapeDtypeStruct((DA_HEADS, m, LANES), BF16)] * 2
                  + [jax.ShapeDtypeStruct((DA_HEADS, m // bm, VT_ROWS, bm), BF16)]
                  + [jax.ShapeDtypeStruct((m, GROUP_W), F32)] * 4,
        compiler_params=_params(("parallel",)),
        name="in_proj",
    )(x, g, w_bf16, *tabs)


def _rope_tables(pos, dim):
    inv = 1.0 / (ROPE_THETA ** (np.arange(0, dim, 2, dtype=np.float64) / dim))
    ang = np.asarray(pos, np.float64)[:, None] * inv[None, :]
    c, s = np.cos(ang), np.sin(ang)
    reps = LANES // dim
    return (np.tile(np.concatenate([c, c], axis=-1), (1, reps)).astype(np.float32),
            np.tile(np.concatenate([-s, s], axis=-1), (1, reps)).astype(np.float32))


def _dattn_prompt_kernel(q_ref, k_ref, vt_ref, lq1, lk1, lq2, lk2,
                         o_ref, qs_ref, sa_ref, sb_ref, m_ref, acc_ref, *, lam_init):
    qi = pl.program_id(2)
    bq = q_ref.shape[0]
    bk = vt_ref.shape[2]

    qt = q_ref[...].astype(F32).T
    row = lax.broadcasted_iota(jnp.int32, qt.shape, 0)
    qs_ref[...] = jnp.concatenate([jnp.where(row < DA_HEAD_DIM, qt, 0.0),
                                   jnp.where(row >= DA_HEAD_DIM, qt, 0.0)], axis=1).astype(BF16)
    m_ref[...] = jnp.full(m_ref.shape, -jnp.inf, F32)
    acc_ref[...] = jnp.zeros(acc_ref.shape, F32)

    def scores(kj):
        return _dot(k_ref[pl.ds(pl.multiple_of(kj * bk, bk), bk), :], qs_ref[...])

    def consume(s_ref, kj, diagonal):
        s = s_ref[...]
        if diagonal:
            kpos = lax.broadcasted_iota(jnp.int32, s.shape, 0)
            qpos = lax.broadcasted_iota(jnp.int32, s.shape, 1) & (bq - 1)
            s = jnp.where(kpos <= qpos, s, -jnp.inf)
        m_old = m_ref[...]
        m_new = jnp.maximum(m_old, jnp.max(s, axis=0, keepdims=True))
        alpha = jnp.exp2(m_old - m_new)
        p = jnp.exp2(s - m_new)
        acc_ref[...] = alpha * acc_ref[...] + _dot(vt_ref[kj], p.astype(BF16))
        m_ref[...] = m_new

    sa_ref[...] = scores(0)

    def pair(i, carry):
        kj = 2 * i
        sb_ref[...] = scores(kj + 1)
        consume(sa_ref, kj, False)
        sa_ref[...] = scores(kj + 2)
        consume(sb_ref, kj + 1, False)
        return carry

    lax.fori_loop(0, qi >> 1, pair, 0)

    @pl.when((qi & 1) == 1)
    def _():
        sb_ref[...] = scores(qi)
        consume(sa_ref, qi - 1, False)
        consume(sb_ref, qi, True)

    @pl.when((qi & 1) == 0)
    def _():
        consume(sa_ref, qi, True)

    lam = _diff_lambda(lq1, lk1, lq2, lk2, lam_init)
    o = acc_ref[:DA_V_DIM, :] / acc_ref[DA_V_DIM:DA_V_DIM + 1, :]
    o_ref[...] = (o[:, :bq] - lam * o[:, bq:]).T


def _dattn_prompt(qh, kh, vt, lams, lam_init, b, t):
    blk = vt.shape[3]
    nq = t // blk
    assert blk & (blk - 1) == 0
    lspec = pl.BlockSpec((1, DA_HEAD_DIM), lambda *_: (0, 0))
    return pl.pallas_call(
        functools.partial(_dattn_prompt_kernel, lam_init=lam_init),
        grid=(b, DA_HEADS, nq),
        in_specs=[pl.BlockSpec((None, blk, LANES), lambda bi, h, qi: (h, bi * nq + qi, 0)),
                  pl.BlockSpec((None, t, LANES), lambda bi, h, qi: (h, bi, 0)),
                  pl.BlockSpec((None, nq, VT_ROWS, blk), lambda bi, h, qi: (h, bi, 0, 0))]
                 + [lspec] * 4,
        out_specs=pl.BlockSpec((blk, LANES), lambda bi, h, qi: (bi * nq + qi, h)),
        out_shape=jax.ShapeDtypeStruct((b * t, DA_HEADS * DA_V_DIM), F32),
        scratch_shapes=[pltpu.VMEM((LANES, 2 * blk), BF16),
                        pltpu.VMEM((blk, 2 * blk), F32),
                        pltpu.VMEM((blk, 2 * blk), F32),
                        pltpu.VMEM((1, 2 * blk), F32),
                        pltpu.VMEM((VT_ROWS, 2 * blk), F32)],
        compiler_params=_params(("parallel", "parallel", "arbitrary")),
        name="dattn_prompt",
    )(qh, kh, vt, *lams)


def _ret_decays(cq, ck, lg, c_len):
    i = lax.broadcasted_iota(jnp.int32, (cq, ck), 0)
    j = lax.broadcasted_iota(jnp.int32, (cq, ck), 1)
    diff = (i - j).astype(F32)
    decay = jnp.where(diff >= 0, jnp.exp(jnp.maximum(diff, 0.0) * lg), 0.0)
    iq = lax.broadcasted_iota(jnp.int32, (cq, 1), 0).astype(F32)
    q_decay = jnp.exp((iq + 1.0) * lg)
    ik = lax.broadcasted_iota(jnp.int32, (ck, 1), 0).astype(F32)
    k_decay = jnp.where(ik < c_len, jnp.exp((c_len - 1.0 - ik) * lg), 0.0)
    return decay, q_decay, k_decay


def _ret_chunks(chains):
    qb = [c[0].astype(BF16) for c in chains]
    vb = [c[2].astype(BF16) for c in chains]
    inner = [_dot_nt(qb[i], c[1].astype(BF16)) for i, c in enumerate(chains)]
    carried = [_dot(qb[i], c[3].astype(BF16)) for i, c in enumerate(chains)]
    update = [_dot_tn((c[1] * c[4][2]).astype(BF16), vb[i]) for i, c in enumerate(chains)]
    out = []
    for i, (_, _, _, state, (decay, q_decay, _), state_decay) in enumerate(chains):
        o = _dot((inner[i] * decay).astype(BF16), vb[i]) + carried[i] * q_decay
        out.append((o, state_decay * state + update[i]))
    return out


def _ret_prompt_kernel(q_ref, k_ref, v_ref, o_ref, s_ref):
    @pl.when(pl.program_id(0) == 0)
    def _():
        s_ref[...] = jnp.zeros(s_ref.shape, F32)

    chains, where = [], []
    for h in range(RET_HEADS):
        sl = slice(h * LANES, (h + 1) * LANES)
        decays = _ret_decays(RET_CHUNK, RET_CHUNK, LOG_GAMMA[h], RET_CHUNK)
        state_decay = math.exp(RET_CHUNK * LOG_GAMMA[h])
        for b in range(q_ref.shape[0]):
            chains.append((q_ref[b, :, sl], k_ref[b, :, sl], v_ref[b, :, sl],
                           s_ref[b, h], decays, state_decay))
            where.append((b, h, sl))
    for (b, h, sl), (o, s_new) in zip(where, _ret_chunks(chains)):
        o_ref[b, :, sl] = o
        s_ref[b, h] = s_new


def _ret_prompt(rq, rk, rv):
    b, t, w = rq.shape
    spec = pl.BlockSpec((b, RET_CHUNK, w), lambda c: (0, c, 0))
    return pl.pallas_call(
        _ret_prompt_kernel,
        grid=(t // RET_CHUNK,),
        in_specs=[spec] * 3,
        out_specs=[spec, pl.BlockSpec((b, RET_HEADS, RET_QK_DIM, RET_V_DIM),
                                      lambda c: (0, 0, 0, 0))],
        out_shape=[jax.ShapeDtypeStruct((b, t, w), F32),
                   jax.ShapeDtypeStruct((b, RET_HEADS, RET_QK_DIM, RET_V_DIM), F32)],
        compiler_params=_params(("arbitrary",)),
        name="ret_prompt",
    )(rq, rk, rv)


def _ret_sample_kernel(q_ref, k_ref, v_ref, s_ref, o_ref, sn_ref):
    c_len = q_ref.shape[1]
    pad = jnp.zeros((LANES - c_len, LANES), F32)
    chains, where = [], []
    for h in range(RET_HEADS):
        sl = slice(h * LANES, (h + 1) * LANES)
        decays = _ret_decays(c_len, LANES, LOG_GAMMA[h], c_len)
        state_decay = math.exp(c_len * LOG_GAMMA[h])
        for b in range(q_ref.shape[0]):
            k = jnp.concatenate([k_ref[b, :, sl], pad], axis=0)
            v = jnp.concatenate([v_ref[b, :, sl], pad], axis=0)
            chains.append((q_ref[b, :, sl], k, v, s_ref[b, h], decays, state_decay))
            where.append((b, h, sl))
    for (b, h, sl), (o, s_new) in zip(where, _ret_chunks(chains)):
        o_ref[b, :, sl] = o
        sn_ref[b, h] = s_new


def _ret_sample(rq, rk, rv, state):
    b, s, w = rq.shape
    rb = next(n for n in (8, 4, 2, 1) if b % n == 0)
    spec = pl.BlockSpec((rb, s, w), lambda bi: (bi, 0, 0))
    sspec = pl.BlockSpec((rb, RET_HEADS, RET_QK_DIM, RET_V_DIM), lambda bi: (bi, 0, 0, 0))
    return pl.pallas_call(
        _ret_sample_kernel,
        grid=(b // rb,),
        in_specs=[spec] * 3 + [sspec],
        out_specs=[spec, sspec],
        out_shape=[jax.ShapeDtypeStruct((b, s, w), F32),
                   jax.ShapeDtypeStruct(state.shape, F32)],
        compiler_params=_params(("parallel",)),
        name="ret_sample",
    )(rq, rk, rv, state)


def _merge_kernel(da_ref, ro_ref, rg_ref, x_ref, gs_ref, gr_ref, w_ref, gc_ref, wq_ref,
                  y_ref, q_ref, *, lam_init):
    parts = []
    for h in range(DA_HEADS):
        sl = slice(h * LANES, (h + 1) * LANES)
        parts.append(_rms(da_ref[:, sl], gs_ref[...]) * (1.0 - lam_init))
    for h in range(RET_HEADS):
        sl = slice(h * LANES, (h + 1) * LANES)
        g = rg_ref[:, sl]
        parts.append(_rms(ro_ref[:, sl], gr_ref[...]) * (g * (1.0 / (1.0 + jnp.exp(-g)))))
    mix = jnp.concatenate(parts, axis=-1).astype(BF16)
    y = x_ref[...] + _dot(mix, w_ref[...])
    y_ref[...] = y
    hq = _rms(y, gc_ref[...]).astype(BF16)
    q_ref[...] = (_dot(hq, wq_ref[...]) * (CA_HEAD_DIM ** -0.5)).astype(BF16)


def _merge(da, ro, rg, x, g_sub, g_ret, w_bf16, g_cross, wq_bf16, lam_init, bm):
    m = x.shape[0]
    row = lambda i: (i, 0)
    spec = pl.BlockSpec((bm, D_MODEL), row)
    return pl.pallas_call(
        functools.partial(_merge_kernel, lam_init=lam_init),
        grid=(m // bm,),
        in_specs=[pl.BlockSpec((bm, GROUP_W), row)] * 3
                 + [spec, _const_spec((1, LANES)), _const_spec((1, LANES)),
                    _const_spec((D_MODEL, D_MODEL)), _const_spec((1, D_MODEL)),
                    _const_spec((D_MODEL, D_MODEL))],
        out_specs=[spec, spec],
        out_shape=[jax.ShapeDtypeStruct((m, D_MODEL), F32),
                   jax.ShapeDtypeStruct((m, D_MODEL), BF16)],
        compiler_params=_params(("parallel",)),
        name="merge_out",
    )(da, ro, rg, x, g_sub, g_ret, w_bf16, g_cross, wq_bf16)


def _mem_kv_kernel(x_ref, g_ref, wk_ref, wv_ref, k_ref, v_ref):
    h = _rms(x_ref[...], g_ref[...]).astype(BF16)
    k = _dot(h, wk_ref[...])
    v = _dot(h, wv_ref[...])
    for hd in range(CA_HEADS):
        sl = slice(hd * CA_HEAD_DIM, (hd + 1) * CA_HEAD_DIM)
        k_ref[:, hd, :] = k[:, sl]
        v_ref[:, hd, :] = v[:, sl]


def _mem_kv(x, g, wk_bf16, wv_bf16, bm):
    m = x.shape[0]
    spec = pl.BlockSpec((bm, D_MODEL), lambda i: (i, 0))
    ospec = pl.BlockSpec((bm, CA_HEADS, CA_HEAD_DIM), lambda i: (i, 0, 0))
    return pl.pallas_call(
        _mem_kv_kernel,
        grid=(m // bm,),
        in_specs=[spec, _const_spec((1, D_MODEL)),
                  _const_spec((D_MODEL, D_MODEL)), _const_spec((D_MODEL, D_MODEL))],
        out_specs=[ospec, ospec],
        out_shape=[jax.ShapeDtypeStruct((m, CA_HEADS, CA_HEAD_DIM), F32)] * 2,
        compiler_params=_params(("parallel",)),
        name="mem_kv",
    )(x, g, wk_bf16, wv_bf16)


def _cross_prompt_kernel(q_ref, mk_ref, mv_ref, o_ref, kh_ref, vh_ref):
    @pl.when(pl.program_id(1) == 0)
    def _():
        for hd in range(CA_HEADS):
            kh_ref[hd] = mk_ref[:, hd, :].astype(BF16)
            vh_ref[hd] = mv_ref[:, hd, :].astype(BF16)

    heads = range(CA_HEADS)
    cols = [slice(hd * CA_HEAD_DIM, (hd + 1) * CA_HEAD_DIM) for hd in heads]
    s = [_dot_nt(q_ref[0, :, cols[hd]], kh_ref[hd]) for hd in heads]
    p = [jnp.exp(x - jnp.max(x, axis=-1, keepdims=True)) for x in s]
    a = [(x / jnp.sum(x, axis=-1, keepdims=True)).astype(BF16) for x in p]
    o = [_dot(a[hd], vh_ref[hd]) for hd in heads]
    for hd in heads:
        o_ref[0, :, cols[hd]] = o[hd].astype(BF16)


def _cross_prompt(q, mk, mv, layer, bt):
    b, t, _ = q.shape
    n_mem = mk.shape[2]
    qspec = pl.BlockSpec((1, bt, D_MODEL), lambda bi, ti: (bi, ti, 0))
    mspec = pl.BlockSpec((None, None, n_mem, CA_HEADS, CA_HEAD_DIM),
                         lambda bi, ti: (layer, bi, 0, 0, 0))
    return pl.pallas_call(
        _cross_prompt_kernel,
        grid=(b, t // bt),
        in_specs=[qspec, mspec, mspec],
        out_specs=qspec,
        out_shape=jax.ShapeDtypeStruct(q.shape, BF16),
        scratch_shapes=[pltpu.VMEM((CA_HEADS, n_mem, CA_HEAD_DIM), BF16)] * 2,
        compiler_params=_params(("parallel", "arbitrary")),
        name="cross_prompt",
    )(q, mk, mv)


def _cross_sample_kernel(q_ref, mk_ref, mv_ref, o_ref):
    n_req, n_mem = mk_ref.shape[:2]
    s_len = q_ref.shape[0] // n_req
    q_all = q_ref[...].astype(F32)
    reqs = range(n_req)
    rows = n_mem * CA_HEADS
    q2 = [jnp.concatenate([q_all[b * s_len:(b + 1) * s_len, hd * CA_HEAD_DIM:(hd + 1) * CA_HEAD_DIM]
                           for hd in range(CA_HEADS)], axis=0).astype(BF16) for b in reqs]
    s = [_dot_nt(q2[b], mk_ref[b].reshape(rows, CA_HEAD_DIM).astype(BF16)) for b in reqs]
    r = lax.broadcasted_iota(jnp.int32, s[0].shape, 0)
    c = lax.broadcasted_iota(jnp.int32, s[0].shape, 1)
    same_head = (c & (CA_HEADS - 1)) == (r >> 3)
    s = [jnp.where(same_head, x, -jnp.inf) for x in s]
    p = [jnp.exp(x - jnp.max(x, axis=-1, keepdims=True)) for x in s]
    a = [(x / jnp.sum(x, axis=-1, keepdims=True)).astype(BF16) for x in p]
    o2 = [_dot(a[b], mv_ref[b].reshape(rows, CA_HEAD_DIM).astype(BF16)) for b in reqs]
    o_ref[...] = jnp.concatenate(
        [jnp.concatenate([o2[b][hd * s_len:(hd + 1) * s_len] for hd in range(CA_HEADS)], axis=1)
         for b in reqs], axis=0).astype(BF16)


def _cross_sample(q, mk, mv, layer, s_len):
    assert s_len == 8 and CA_HEADS == 4
    n_mem = mk.shape[2]
    b = q.shape[0] // s_len
    rb = next(n for n in (8, 4, 2, 1) if b % n == 0)
    qspec = pl.BlockSpec((rb * s_len, D_MODEL), lambda bi: (bi, 0))
    mspec = pl.BlockSpec((None, rb, n_mem, CA_HEADS, CA_HEAD_DIM),
                         lambda bi: (layer, bi, 0, 0, 0))
    return pl.pallas_call(
        _cross_sample_kernel,
        grid=(b // rb,),
        in_specs=[qspec, mspec, mspec],
        out_specs=qspec,
        out_shape=jax.ShapeDtypeStruct(q.shape, BF16),
        compiler_params=_params(("parallel",)),
        name="cross_sample",
    )(q, mk, mv)


def _mlp_kernel(y_ref, a_ref, wo_ref, g_ref, wu_ref, wd_ref, gf_ref, o_ref, *, ff_chunk):
    y = y_ref[...] + _dot(a_ref[...], wo_ref[...])
    h = _rms(y, g_ref[...]).astype(BF16)
    acc = y
    for c in range(D_FF // ff_chunk):
        sl = slice(c * ff_chunk, (c + 1) * ff_chunk)
        u = jnp.maximum(_dot(h, wu_ref[:, sl]), 0.0)
        acc = acc + _dot((u * u).astype(BF16), wd_ref[sl, :])
    o_ref[...] = _rms(acc, gf_ref[...])


def _mlp(y, attn, wo_bf16, g, wu_bf16, wd_bf16, g_final, bm):
    m = y.shape[0]
    row = lambda i: (i, 0)
    spec = pl.BlockSpec((bm, D_MODEL), row)
    return pl.pallas_call(
        functools.partial(_mlp_kernel, ff_chunk=1024),
        grid=(m // bm,),
        in_specs=[spec, spec, _const_spec((D_MODEL, D_MODEL)), _const_spec((1, D_MODEL)),
                  _const_spec((D_MODEL, D_FF)), _const_spec((D_FF, D_MODEL)),
                  _const_spec((1, D_MODEL))],
        out_specs=spec,
        out_shape=jax.ShapeDtypeStruct((m, D_MODEL), F32),
        compiler_params=_params(("parallel",)),
        name="mlp_final",
    )(y, attn, wo_bf16, g, wu_bf16, wd_bf16, g_final)


def _dattn_sample_kernel(pt_ref, q_ref, kn_ref, vn_ref, lq1, lk1, lq2, lk2, *rest,
                         pages_per_step, lam_init):
    del pt_ref
    k_refs = rest[:pages_per_step]
    v_refs = rest[pages_per_step:2 * pages_per_step]
    o_ref, qs_ref, m_ref, l_ref, acc_ref = rest[2 * pages_per_step:]
    j = pl.program_id(1)
    s_len = q_ref.shape[1]
    n_rows = 2 * DA_HEADS * s_len
    page_rows = k_refs[0].shape[0]

    def same_head(r, c):
        return (c & (DA_HEADS - 1)) == (r >> 4)

    @pl.when(j == 0)
    def _():
        parts = []
        lane = lax.broadcasted_iota(jnp.int32, (s_len, LANES), 1)
        for hd in range(DA_HEADS):
            q = q_ref[hd].astype(F32)
            parts += [jnp.where(lane < DA_HEAD_DIM, q, 0.0), jnp.where(lane >= DA_HEAD_DIM, q, 0.0)]
        qs = jnp.concatenate(parts, axis=0).astype(BF16)
        qs_ref[...] = qs
        pad = jnp.zeros((LANES - DA_HEADS * s_len, LANES), F32)
        kn = jnp.concatenate([kn_ref[...], pad], axis=0).astype(BF16)
        vn = jnp.concatenate([vn_ref[...], pad], axis=0).astype(BF16)
        s = _dot_nt(qs, kn)
        r = lax.broadcasted_iota(jnp.int32, s.shape, 0)
        c = lax.broadcasted_iota(jnp.int32, s.shape, 1)
        causal = (c >> 2) <= (r & (s_len - 1))
        s = jnp.where(same_head(r, c) & causal, s, -jnp.inf)
        m = jnp.max(s, axis=-1, keepdims=True)
        p = jnp.exp2(s - m)
        m_ref[...] = m
        l_ref[...] = jnp.sum(p, axis=-1, keepdims=True)
        acc_ref[...] = _dot(p.astype(BF16), vn)

    def head_rows(refs, hd):
        return jnp.concatenate(
            [r[pl.ds(hd, page_rows // DA_HEADS, stride=DA_HEADS), :].astype(BF16) for r in refs],
            axis=0)

    hr = n_rows // DA_HEADS
    s = jnp.concatenate([_dot_nt(qs_ref[hd * hr:(hd + 1) * hr, :], head_rows(k_refs, hd))
                         for hd in range(DA_HEADS)], axis=0)
    m_old = m_ref[...]
    m_new = jnp.maximum(m_old, jnp.max(s, axis=-1, keepdims=True))
    alpha = jnp.exp2(m_old - m_new)
    p = jnp.exp2(s - m_new)
    l_ref[...] = alpha * l_ref[...] + jnp.sum(p, axis=-1, keepdims=True)
    pv = jnp.concatenate([_dot(p[hd * hr:(hd + 1) * hr].astype(BF16), head_rows(v_refs, hd))
                          for hd in range(DA_HEADS)], axis=0)
    acc_ref[...] = alpha * acc_ref[...] + pv
    m_ref[...] = m_new

    @pl.when(j == pl.num_programs(1) - 1)
    def _():
        lam = _diff_lambda(lq1, lk1, lq2, lk2, lam_init)
        o = acc_ref[...] / l_ref[...]
        for h in range(DA_HEADS):
            r0 = 2 * h * s_len
            o_ref[:, h * LANES:(h + 1) * LANES] = (o[r0:r0 + s_len]
                                                   - lam * o[r0 + s_len:r0 + 2 * s_len])


def _dattn_sample(qh, kf, vf, cache_k, cache_v, layer, page_table, lams, lam_init, pages_per_step):
    b, n_pages = page_table.shape
    s = qh.shape[1] // b
    assert s == 8 and DA_HEADS == 4
    depth, n_pool, page = cache_k.shape[:3]
    page_rows = page * DA_HEADS
    ck = cache_k.reshape(depth, n_pool, page_rows, LANES)
    cv = cache_v.reshape(depth, n_pool, page_rows, LANES)
    pt = page_table.reshape(-1)
    steps = n_pages // pages_per_step
    qspec = pl.BlockSpec((DA_HEADS, s, LANES), lambda bi, j, pt_r: (0, bi, 0))
    nspec = pl.BlockSpec((s * DA_HEADS, LANES), lambda bi, j, pt_r: (bi, 0))
    lspec = pl.BlockSpec((1, DA_HEAD_DIM), lambda *_: (0, 0))

    def page_spec(p):
        return pl.BlockSpec(
            (None, None, page_rows, LANES),
            lambda bi, j, pt_r: (layer, pt_r[bi * n_pages + j * pages_per_step + p], 0, 0))

    n_rows = 2 * DA_HEADS * s
    return pl.pallas_call(
        functools.partial(_dattn_sample_kernel, pages_per_step=pages_per_step,
                          lam_init=lam_init),
        grid_spec=pltpu.PrefetchScalarGridSpec(
            num_scalar_prefetch=1,
            grid=(b, steps),
            in_specs=[qspec, nspec, nspec] + [lspec] * 4
                     + [page_spec(p) for p in range(pages_per_step)] * 2,
            out_specs=pl.BlockSpec((s, DA_HEADS * LANES), lambda bi, j, pt_r: (bi, 0)),
            scratch_shapes=[pltpu.VMEM((n_rows, LANES), BF16),
                            pltpu.VMEM((n_rows, 1), F32),
                            pltpu.VMEM((n_rows, 1), F32),
                            pltpu.VMEM((n_rows, LANES), F32)]),
        out_shape=jax.ShapeDtypeStruct((b * s, DA_HEADS * LANES), F32),
        compiler_params=_params(("parallel", "arbitrary")),
        name="dattn_sample",
    )(pt, qh, kf, vf, *lams, *([ck] * pages_per_step), *([cv] * pages_per_step))


def kernel(x_prompt, x_sample, cache_k, cache_v, state_ret, cache_mem_k, cache_mem_v, page_table, mem_prompt, g_mix, w_in, lambda_q1, lambda_k1, lambda_q2, lambda_k2, g_diff_sub, g_ret, w_out, g_cross, g_mem, w_cq, w_ck, w_cv, w_co, g_mlp, w_up, w_down, g_final):
    bp, t, d = x_prompt.shape
    bd, s, _ = x_sample.shape
    depth = w_in.shape[0]
    assert depth == 1, "one pass over the final norm per layer stack of depth 1"
    n_mem = mem_prompt.shape[1]
    past = page_table.shape[1] * cache_k.shape[2]
    l = 0
    lam_init = 0.8 - 0.6 * math.exp(-0.3 * l)

    bm_p = 512
    tabs_p = _rope_tables(np.arange(t), DA_HEAD_DIM) + _rope_tables(np.arange(t), RET_QK_DIM)
    bm_s = min(256, bd * s)
    pos_s = past + np.arange(s)
    tabs_s = tuple(np.tile(tb, (bm_s // s, 1)) for tb in
                   _rope_tables(pos_s, DA_HEAD_DIM) + _rope_tables(pos_s, RET_QK_DIM))

    row2 = lambda a: a.reshape(1, -1)
    bf = lambda a: a.astype(BF16)
    w_in_b, w_out_b = bf(w_in[l]), bf(w_out[l])
    w_cq_b, w_ck_b, w_cv_b, w_co_b = bf(w_cq[l]), bf(w_ck[l]), bf(w_cv[l]), bf(w_co[l])
    w_up_b, w_down_b = bf(w_up[l]), bf(w_down[l])
    lams = (row2(lambda_q1[l]), row2(lambda_k1[l]), row2(lambda_q2[l]), row2(lambda_k2[l]))
    g_mix_l, g_cross_l, g_mlp_l = row2(g_mix[l]), row2(g_cross[l]), row2(g_mlp[l])
    g_sub_l, g_ret_l, g_mem_l, g_fin = row2(g_diff_sub[l]), row2(g_ret[l]), row2(g_mem[l]), row2(g_final)

    xp = x_prompt.reshape(bp * t, d)
    kf, vf, qh, kh, vt, rq, rk, rv, rg = _in_proj(xp, g_mix_l, w_in_b, tabs_p, bm_p, t // bm_p)
    seq = lambda a: a.reshape(bp, t, GROUP_W)
    da = _dattn_prompt(qh, kh, vt, lams, lam_init, bp, t)
    ro, s_p = _ret_prompt(seq(rq), seq(rk), seq(rv))
    y1, cq = _merge(da, ro.reshape(bp * t, GROUP_W), rg, xp,
                    g_sub_l, g_ret_l, w_out_b, g_cross_l, w_cq_b, lam_init, bm_p)
    mk, mv = _mem_kv(mem_prompt.reshape(bp * n_mem, d), g_mem_l, w_ck_b, w_cv_b, 256)
    mk = mk.reshape(1, bp, n_mem, CA_HEADS, CA_HEAD_DIM)
    mv = mv.reshape(1, bp, n_mem, CA_HEADS, CA_HEAD_DIM)
    ca = _cross_prompt(cq.reshape(bp, t, d), mk, mv, 0, 512).reshape(bp * t, d)
    y_prompt = _mlp(y1, ca, w_co_b, g_mlp_l, w_up_b, w_down_b, g_fin, bm_p).reshape(bp, t, d)

    xs = x_sample.reshape(bd * s, d)
    kfs, vfs, qh, _, _, rq, rk, rv, rg = _in_proj(xs, g_mix_l, w_in_b, tabs_s, bm_s, 1)
    req = lambda a: a.reshape(bd, s, GROUP_W)
    da = _dattn_sample(qh, kfs, vfs, cache_k, cache_v, l, page_table, lams, lam_init, 32)
    ro, s_s = _ret_sample(req(rq), req(rk), req(rv), state_ret[l])
    y1, cq = _merge(da, ro.reshape(bd * s, GROUP_W), rg, xs,
                    g_sub_l, g_ret_l, w_out_b, g_cross_l, w_cq_b, lam_init, bm_s)
    ca = _cross_sample(cq, cache_mem_k, cache_mem_v, l, s)
    y_sample = _mlp(y1, ca, w_co_b, g_mlp_l, w_up_b, w_down_b, g_fin, bm_s).reshape(bd, s, d)

    return (y_prompt, y_sample,
            kf.reshape(1, bp, t, DA_HEADS, 2 * DA_HEAD_DIM),
            vf.reshape(1, bp, t, DA_HEADS, DA_V_DIM),
            s_p[None], mk, mv,
            kfs.reshape(1, bd, s, DA_HEADS, 2 * DA_HEAD_DIM),
            vfs.reshape(1, bd, s, DA_HEADS, DA_V_DIM),
            s_s[None])
```

```python
import functools
import math

import jax
import jax.numpy as jnp
import numpy as np
from jax import lax
from jax.experimental import pallas as pl
from jax.experimental.pallas import tpu as pltpu

F32 = jnp.float32
BF16 = jnp.bfloat16

D_MODEL = 1024
DA_HEADS = 4
DA_HEAD_DIM = 64
DA_V_DIM = 128
RET_HEADS = 4
RET_QK_DIM = 128
RET_V_DIM = 128
CA_HEADS = 4
CA_HEAD_DIM = 256
D_FF = 4096
GROUP_W = 512
N_GROUPS = 7
ROPE_THETA = 10000.0
RMS_EPS = 1e-6
RET_CHUNK = 128
LANES = 128
BF16_SUBLANES = 16
PAGE_SLOTS = 3
VT_ROWS = DA_V_DIM + BF16_SUBLANES
Q_SCALE = DA_HEAD_DIM ** -0.5 * math.log2(math.e)
VMEM_LIMIT = 56 * 1024 * 1024

LOG_GAMMA = tuple(math.log(1.0 - 2.0 ** (-5.0 - h)) for h in range(RET_HEADS))


def _dot(a, b):
    return jnp.dot(a, b, preferred_element_type=F32)


def _dot_nt(a, b):
    return lax.dot_general(a, b, (((1,), (1,)), ((), ())), preferred_element_type=F32)


def _dot_tn(a, b):
    return lax.dot_general(a, b, (((0,), (0,)), ((), ())), preferred_element_type=F32)


def _rms(x, g):
    ms = jnp.mean(x * x, axis=-1, keepdims=True)
    return x * lax.rsqrt(ms + RMS_EPS) * g


def _diff_lambda(lq1, lk1, lq2, lk2, lam_init):
    a = jnp.sum(lq1[...] * lk1[...], axis=-1, keepdims=True)
    b = jnp.sum(lq2[...] * lk2[...], axis=-1, keepdims=True)
    return jnp.exp(a) - jnp.exp(b) + lam_init


def _params(sem):
    return pltpu.CompilerParams(dimension_semantics=sem, vmem_limit_bytes=VMEM_LIMIT)


def _const_spec(shape):
    nd = len(shape)
    return pl.BlockSpec(shape, lambda *_: (0,) * nd, pipeline_mode=pl.Buffered(1))


def _in_proj_kernel(x_ref, g_ref, w_ref, cd_ref, sd_ref, cr_ref, sr_ref,
                    kf_ref, vf_ref, qh_ref, kh_ref, vt_ref, rq_ref, rk_ref, rv_ref, rg_ref):
    h = _rms(x_ref[...], g_ref[...]).astype(BF16)
    bm = h.shape[0]
    lane = lax.broadcasted_iota(jnp.int32, (bm, LANES), 1)
    first_half = (lane & (DA_HEAD_DIM - 1)) < (DA_HEAD_DIM // 2)
    cd, sd, cr, sr = cd_ref[...], sd_ref[...], cr_ref[...], sr_ref[...]

    def proj(g):
        return _dot(h, w_ref[:, g * GROUP_W:(g + 1) * GROUP_W])

    def rope_da(x):
        sw = jnp.where(first_half, pltpu.roll(x, LANES - 32, 1), pltpu.roll(x, 32, 1))
        return x * cd + sw * sd

    def rope_ret(x):
        return x * cr + pltpu.roll(x, 64, 1) * sr

    def store(ref, x, fn):
        for c in range(GROUP_W // LANES):
            sl = slice(c * LANES, (c + 1) * LANES)
            ref[:, sl] = fn(x[:, sl])

    dq, dk, dv = proj(0), proj(1), proj(2)
    for hd in range(DA_HEADS):
        sl = slice(hd * LANES, (hd + 1) * LANES)
        rows = pl.ds(hd, bm, stride=DA_HEADS)
        qh_ref[hd] = (rope_da(dq[:, sl]) * Q_SCALE).astype(BF16)
        k = rope_da(dk[:, sl])
        kf_ref[rows, :] = k
        kh_ref[hd] = k.astype(BF16)
        v = dv[:, sl]
        vf_ref[rows, :] = v
        vt_ref[hd, 0, :DA_V_DIM, :] = v.T.astype(BF16)
        vt_ref[hd, 0, DA_V_DIM:, :] = jnp.ones((VT_ROWS - DA_V_DIM, bm), BF16)
    store(rq_ref, proj(3), rope_ret)
    store(rk_ref, proj(4), lambda x: rope_ret(x) * (RET_QK_DIM ** -0.5))
    rv_ref[...] = proj(5)
    rg_ref[...] = proj(6)


def _in_proj(x, g, w_bf16, tabs, bm, tab_blocks):
    m = x.shape[0]
    row = lambda i: (i, 0)
    tab = lambda i: (i % tab_blocks, 0)
    wide = pl.BlockSpec((bm, GROUP_W), row)
    inter = pl.BlockSpec((bm * DA_HEADS, LANES), row)
    head = pl.BlockSpec((DA_HEADS, bm, LANES), lambda i: (0, i, 0))
    head_t = pl.BlockSpec((DA_HEADS, 1, VT_ROWS, bm), lambda i: (0, i, 0, 0))
    return pl.pallas_call(
        _in_proj_kernel,
        grid=(m // bm,),
        in_specs=[pl.BlockSpec((bm, D_MODEL), row),
                  _const_spec((1, D_MODEL)),
                  _const_spec((D_MODEL, N_GROUPS * GROUP_W))]
                 + [pl.BlockSpec((bm, LANES), tab)] * 4,
        out_specs=[inter, inter, head, head, head_t, wide, wide, wide, wide],
        out_shape=[jax.ShapeDtypeStruct((m * DA_HEADS, LANES), F32)] * 2
                  + [jax.ShapeDtypeStruct((DA_HEADS, m, LANES), BF16)] * 2
                  + [jax.ShapeDtypeStruct((DA_HEADS, m // bm, VT_ROWS, bm), BF16)]
                  + [jax.ShapeDtypeStruct((m, GROUP_W), F32)] * 4,
        compiler_params=_params(("parallel",)),
        name="in_proj",
    )(x, g, w_bf16, *tabs)


def _rope_tables(pos, dim):
    inv = 1.0 / (ROPE_THETA ** (np.arange(0, dim, 2, dtype=np.float64) / dim))
    ang = np.asarray(pos, np.float64)[:, None] * inv[None, :]
    c, s = np.cos(ang), np.sin(ang)
    reps = LANES // dim
    return (np.tile(np.concatenate([c, c], axis=-1), (1, reps)).astype(np.float32),
            np.tile(np.concatenate([-s, s], axis=-1), (1, reps)).astype(np.float32))


def _dattn_prompt_kernel(q_ref, k_ref, vt_ref, lq1, lk1, lq2, lk2,
                         o_ref, qs_ref, sa_ref, sb_ref, m_ref, acc_ref, *, lam_init):
    qi = pl.program_id(2)
    bq = q_ref.shape[0]
    bk = vt_ref.shape[2]

    qt = q_ref[...].astype(F32).T
    row = lax.broadcasted_iota(jnp.int32, qt.shape, 0)
    qs_ref[...] = jnp.concatenate([jnp.where(row < DA_HEAD_DIM, qt, 0.0),
                                   jnp.where(row >= DA_HEAD_DIM, qt, 0.0)], axis=1).astype(BF16)
    m_ref[...] = jnp.full(m_ref.shape, -jnp.inf, F32)
    acc_ref[...] = jnp.zeros(acc_ref.shape, F32)

    def scores(kj):
        return _dot(k_ref[pl.ds(pl.multiple_of(kj * bk, bk), bk), :], qs_ref[...])

    def consume(s_ref, kj, diagonal):
        s = s_ref[...]
        if diagonal:
            kpos = lax.broadcasted_iota(jnp.int32, s.shape, 0)
            qpos = lax.broadcasted_iota(jnp.int32, s.shape, 1) & (bq - 1)
            s = jnp.where(kpos <= qpos, s, -jnp.inf)
        m_old = m_ref[...]
        m_new = jnp.maximum(m_old, jnp.max(s, axis=0, keepdims=True))
        alpha = jnp.exp2(m_old - m_new)
        p = jnp.exp2(s - m_new)
        acc_ref[...] = alpha * acc_ref[...] + _dot(vt_ref[kj], p.astype(BF16))
        m_ref[...] = m_new

    sa_ref[...] = scores(0)

    def pair(i, carry):
        kj = 2 * i
        sb_ref[...] = scores(kj + 1)
        consume(sa_ref, kj, False)
        sa_ref[...] = scores(kj + 2)
        consume(sb_ref, kj + 1, False)
        return carry

    lax.fori_loop(0, qi >> 1, pair, 0)

    @pl.when((qi & 1) == 1)
    def _():
        sb_ref[...] = scores(qi)
        consume(sa_ref, qi - 1, False)
        consume(sb_ref, qi, True)

    @pl.when((qi & 1) == 0)
    def _():
        consume(sa_ref, qi, True)

    lam = _diff_lambda(lq1, lk1, lq2, lk2, lam_init)
    o = acc_ref[:DA_V_DIM, :] / acc_ref[DA_V_DIM:DA_V_DIM + 1, :]
    o_ref[...] = (o[:, :bq] - lam * o[:, bq:]).T


def _dattn_prompt(qh, kh, vt, lams, lam_init, b, t):
    blk = vt.shape[3]
    nq = t // blk
    assert blk & (blk - 1) == 0
    lspec = pl.BlockSpec((1, DA_HEAD_DIM), lambda *_: (0, 0))
    return pl.pallas_call(
        functools.partial(_dattn_prompt_kernel, lam_init=lam_init),
        grid=(b, DA_HEADS, nq),
        in_specs=[pl.BlockSpec((None, blk, LANES), lambda bi, h, qi: (h, bi * nq + qi, 0)),
                  pl.BlockSpec((None, t, LANES), lambda bi, h, qi: (h, bi, 0)),
                  pl.BlockSpec((None, nq, VT_ROWS, blk), lambda bi, h, qi: (h, bi, 0, 0))]
                 + [lspec] * 4,
        out_specs=pl.BlockSpec((blk, LANES), lambda bi, h, qi: (bi * nq + qi, h)),
        out_shape=jax.ShapeDtypeStruct((b * t, DA_HEADS * DA_V_DIM), F32),
        scratch_shapes=[pltpu.VMEM((LANES, 2 * blk), BF16),
                        pltpu.VMEM((blk, 2 * blk), F32),
                        pltpu.VMEM((blk, 2 * blk), F32),
                        pltpu.VMEM((1, 2 * blk), F32),
                        pltpu.VMEM((VT_ROWS, 2 * blk), F32)],
        compiler_params=_params(("parallel", "parallel", "arbitrary")),
        name="dattn_prompt",
    )(qh, kh, vt, *lams)


def _ret_decays(cq, ck, lg, c_len):
    i = lax.broadcasted_iota(jnp.int32, (cq, ck), 0)
    j = lax.broadcasted_iota(jnp.int32, (cq, ck), 1)
    diff = (i - j).astype(F32)
    decay = jnp.where(diff >= 0, jnp.exp(jnp.maximum(diff, 0.0) * lg), 0.0)
    iq = lax.broadcasted_iota(jnp.int32, (cq, 1), 0).astype(F32)
    q_decay = jnp.exp((iq + 1.0) * lg)
    ik = lax.broadcasted_iota(jnp.int32, (ck, 1), 0).astype(F32)
    k_decay = jnp.where(ik < c_len, jnp.exp((c_len - 1.0 - ik) * lg), 0.0)
    return decay, q_decay, k_decay


def _ret_chunks(chains):
    qb = [c[0].astype(BF16) for c in chains]
    vb = [c[2].astype(BF16) for c in chains]
    inner = [_dot_nt(qb[i], c[1].astype(BF16)) for i, c in enumerate(chains)]
    carried = [_dot(qb[i], c[3].astype(BF16)) for i, c in enumerate(chains)]
    update = [_dot_tn((c[1] * c[4][2]).astype(BF16), vb[i]) for i, c in enumerate(chains)]
    out = []
    for i, (_, _, _, state, (decay, q_decay, _), state_decay) in enumerate(chains):
        o = _dot((inner[i] * decay).astype(BF16), vb[i]) + carried[i] * q_decay
        out.append((o, state_decay * state + update[i]))
    return out


def _ret_prompt_kernel(q_ref, k_ref, v_ref, o_ref, s_ref):
    @pl.when(pl.program_id(0) == 0)
    def _():
        s_ref[...] = jnp.zeros(s_ref.shape, F32)

    chains, where = [], []
    for h in range(RET_HEADS):
        sl = slice(h * LANES, (h + 1) * LANES)
        decays = _ret_decays(RET_CHUNK, RET_CHUNK, LOG_GAMMA[h], RET_CHUNK)
        state_decay = math.exp(RET_CHUNK * LOG_GAMMA[h])
        for b in range(q_ref.shape[0]):
            chains.append((q_ref[b, :, sl], k_ref[b, :, sl], v_ref[b, :, sl],
                           s_ref[b, h], decays, state_decay))
            where.append((b, h, sl))
    for (b, h, sl), (o, s_new) in zip(where, _ret_chunks(chains)):
        o_ref[b, :, sl] = o
        s_ref[b, h] = s_new


def _ret_prompt(rq, rk, rv):
    b, t, w = rq.shape
    spec = pl.BlockSpec((b, RET_CHUNK, w), lambda c: (0, c, 0))
    return pl.pallas_call(
        _ret_prompt_kernel,
        grid=(t // RET_CHUNK,),
        in_specs=[spec] * 3,
        out_specs=[spec, pl.BlockSpec((b, RET_HEADS, RET_QK_DIM, RET_V_DIM),
                                      lambda c: (0, 0, 0, 0))],
        out_shape=[jax.ShapeDtypeStruct((b, t, w), F32),
                   jax.ShapeDtypeStruct((b, RET_HEADS, RET_QK_DIM, RET_V_DIM), F32)],
        compiler_params=_params(("arbitrary",)),
        name="ret_prompt",
    )(rq, rk, rv)


def _ret_sample_kernel(q_ref, k_ref, v_ref, s_ref, o_ref, sn_ref):
    c_len = q_ref.shape[1]
    pad = jnp.zeros((LANES - c_len, LANES), F32)
    chains, where = [], []
    for h in range(RET_HEADS):
        sl = slice(h * LANES, (h + 1) * LANES)
        decays = _ret_decays(c_len, LANES, LOG_GAMMA[h], c_len)
        state_decay = math.exp(c_len * LOG_GAMMA[h])
        for b in range(q_ref.shape[0]):
            k = jnp.concatenate([k_ref[b, :, sl], pad], axis=0)
            v = jnp.concatenate([v_ref[b, :, sl], pad], axis=0)
            chains.append((q_ref[b, :, sl], k, v, s_ref[b, h], decays, state_decay))
            where.append((b, h, sl))
    for (b, h, sl), (o, s_new) in zip(where, _ret_chunks(chains)):
        o_ref[b, :, sl] = o
        sn_ref[b, h] = s_new


def _ret_sample(rq, rk, rv, state):
    b, s, w = rq.shape
    rb = next(n for n in (8, 4, 2, 1) if b % n == 0)
    spec = pl.BlockSpec((rb, s, w), lambda bi: (bi, 0, 0))
    sspec = pl.BlockSpec((rb, RET_HEADS, RET_QK_DIM, RET_V_DIM), lambda bi: (bi, 0, 0, 0))
    return pl.pallas_call(
        _ret_sample_kernel,
        grid=(b // rb,),
        in_specs=[spec] * 3 + [sspec],
        out_specs=[spec, sspec],
        out_shape=[jax.ShapeDtypeStruct((b, s, w), F32),
                   jax.ShapeDtypeStruct(state.shape, F32)],
        compiler_params=_params(("parallel",)),
        name="ret_sample",
    )(rq, rk, rv, state)


def _merge_kernel(da_ref, ro_ref, rg_ref, x_ref, gs_ref, gr_ref, w_ref, gc_ref, wq_ref,
                  y_ref, q_ref, *, lam_init):
    parts = []
    for h in range(DA_HEADS):
        sl = slice(h * LANES, (h + 1) * LANES)
        parts.append(_rms(da_ref[:, sl], gs_ref[...]) * (1.0 - lam_init))
    for h in range(RET_HEADS):
        sl = slice(h * LANES, (h + 1) * LANES)
        g = rg_ref[:, sl]
        parts.append(_rms(ro_ref[:, sl], gr_ref[...]) * (g * (1.0 / (1.0 + jnp.exp(-g)))))
    mix = jnp.concatenate(parts, axis=-1).astype(BF16)
    y = x_ref[...] + _dot(mix, w_ref[...])
    y_ref[...] = y
    hq = _rms(y, gc_ref[...]).astype(BF16)
    q_ref[...] = (_dot(hq, wq_ref[...]) * (CA_HEAD_DIM ** -0.5)).astype(BF16)


def _merge(da, ro, rg, x, g_sub, g_ret, w_bf16, g_cross, wq_bf16, lam_init, bm):
    m = x.shape[0]
    row = lambda i: (i, 0)
    spec = pl.BlockSpec((bm, D_MODEL), row)
    return pl.pallas_call(
        functools.partial(_merge_kernel, lam_init=lam_init),
        grid=(m // bm,),
        in_specs=[pl.BlockSpec((bm, GROUP_W), row)] * 3
                 + [spec, _const_spec((1, LANES)), _const_spec((1, LANES)),
                    _const_spec((D_MODEL, D_MODEL)), _const_spec((1, D_MODEL)),
                    _const_spec((D_MODEL, D_MODEL))],
        out_specs=[spec, spec],
        out_shape=[jax.ShapeDtypeStruct((m, D_MODEL), F32),
                   jax.ShapeDtypeStruct((m, D_MODEL), BF16)],
        compiler_params=_params(("parallel",)),
        name="merge_out",
    )(da, ro, rg, x, g_sub, g_ret, w_bf16, g_cross, wq_bf16)


def _mem_kv_kernel(x_ref, g_ref, wk_ref, wv_ref, k_ref, v_ref):
    h = _rms(x_ref[...], g_ref[...]).astype(BF16)
    k = _dot(h, wk_ref[...])
    v = _dot(h, wv_ref[...])
    for hd in range(CA_HEADS):
        sl = slice(hd * CA_HEAD_DIM, (hd + 1) * CA_HEAD_DIM)
        k_ref[:, hd, :] = k[:, sl]
        v_ref[:, hd, :] = v[:, sl]


def _mem_kv(x, g, wk_bf16, wv_bf16, bm):
    m = x.shape[0]
    spec = pl.BlockSpec((bm, D_MODEL), lambda i: (i, 0))
    ospec = pl.BlockSpec((bm, CA_HEADS, CA_HEAD_DIM), lambda i: (i, 0, 0))
    return pl.pallas_call(
        _mem_kv_kernel,
        grid=(m // bm,),
        in_specs=[spec, _const_spec((1, D_MODEL)),
                  _const_spec((D_MODEL, D_MODEL)), _const_spec((D_MODEL, D_MODEL))],
        out_specs=[ospec, ospec],
        out_shape=[jax.ShapeDtypeStruct((m, CA_HEADS, CA_HEAD_DIM), F32)] * 2,
        compiler_params=_params(("parallel",)),
        name="mem_kv",
    )(x, g, wk_bf16, wv_bf16)


def _cross_prompt_kernel(q_ref, mk_ref, mv_ref, o_ref, kh_ref, vh_ref):
    @pl.when(pl.program_id(1) == 0)
    def _():
        for hd in range(CA_HEADS):
            kh_ref[hd] = mk_ref[:, hd, :].astype(BF16)
            vh_ref[hd] = mv_ref[:, hd, :].astype(BF16)

    heads = range(CA_HEADS)
    cols = [slice(hd * CA_HEAD_DIM, (hd + 1) * CA_HEAD_DIM) for hd in heads]
    s = [_dot_nt(q_ref[0, :, cols[hd]], kh_ref[hd]) for hd in heads]
    p = [jnp.exp(x - jnp.max(x, axis=-1, keepdims=True)) for x in s]
    a = [(x / jnp.sum(x, axis=-1, keepdims=True)).astype(BF16) for x in p]
    o = [_dot(a[hd], vh_ref[hd]) for hd in heads]
    for hd in heads:
        o_ref[0, :, cols[hd]] = o[hd].astype(BF16)


def _cross_prompt(q, mk, mv, layer, bt):
    b, t, _ = q.shape
    n_mem = mk.shape[2]
    qspec = pl.BlockSpec((1, bt, D_MODEL), lambda bi, ti: (bi, ti, 0))
    mspec = pl.BlockSpec((None, None, n_mem, CA_HEADS, CA_HEAD_DIM),
                         lambda bi, ti: (layer, bi, 0, 0, 0))
    return pl.pallas_call(
        _cross_prompt_kernel,
        grid=(b, t // bt),
        in_specs=[qspec, mspec, mspec],
        out_specs=qspec,
        out_shape=jax.ShapeDtypeStruct(q.shape, BF16),
        scratch_shapes=[pltpu.VMEM((CA_HEADS, n_mem, CA_HEAD_DIM), BF16)] * 2,
        compiler_params=_params(("parallel", "arbitrary")),
        name="cross_prompt",
    )(q, mk, mv)


def _cross_sample_kernel(q_ref, mk_ref, mv_ref, o_ref):
    n_req, n_mem = mk_ref.shape[:2]
    s_len = q_ref.shape[0] // n_req
    q_all = q_ref[...].astype(F32)
    reqs = range(n_req)
    rows = n_mem * CA_HEADS
    q2 = [jnp.concatenate([q_all[b * s_len:(b + 1) * s_len, hd * CA_HEAD_DIM:(hd + 1) * CA_HEAD_DIM]
                           for hd in range(CA_HEADS)], axis=0).astype(BF16) for b in reqs]
    s = [_dot_nt(q2[b], mk_ref[b].reshape(rows, CA_HEAD_DIM).astype(BF16)) for b in reqs]
    r = lax.broadcasted_iota(jnp.int32, s[0].shape, 0)
    c = lax.broadcasted_iota(jnp.int32, s[0].shape, 1)
    same_head = (c & (CA_HEADS - 1)) == (r >> 3)
    s = [jnp.where(same_head, x, -jnp.inf) for x in s]
    p = [jnp.exp(x - jnp.max(x, axis=-1, keepdims=True)) for x in s]
    a = [(x / jnp.sum(x, axis=-1, keepdims=True)).astype(BF16) for x in p]
    o2 = [_dot(a[b], mv_ref[b].reshape(rows, CA_HEAD_DIM).astype(BF16)) for b in reqs]
    o_ref[...] = jnp.concatenate(
        [jnp.concatenate([o2[b][hd * s_len:(hd + 1) * s_len] for hd in range(CA_HEADS)], axis=1)
         for b in reqs], axis=0).astype(BF16)


def _cross_sample(q, mk, mv, layer, s_len):
    assert s_len == 8 and CA_HEADS == 4
    n_mem = mk.shape[2]
    b = q.shape[0] // s_len
    rb = next(n for n in (8, 4, 2, 1) if b % n == 0)
    qspec = pl.BlockSpec((rb * s_len, D_MODEL), lambda bi: (bi, 0))
    mspec = pl.BlockSpec((None, rb, n_mem, CA_HEADS, CA_HEAD_DIM),
                         lambda bi: (layer, bi, 0, 0, 0))
    return pl.pallas_call(
        _cross_sample_kernel,
        grid=(b // rb,),
        in_specs=[qspec, mspec, mspec],
        out_specs=qspec,
        out_shape=jax.ShapeDtypeStruct(q.shape, BF16),
        compiler_params=_params(("parallel",)),
        name="cross_sample",
    )(q, mk, mv)


def _mlp_kernel(y_ref, a_ref, wo_ref, g_ref, wu_ref, wd_ref, gf_ref, o_ref, *, ff_chunk):
    y = y_ref[...] + _dot(a_ref[...], wo_ref[...])
    h = _rms(y, g_ref[...]).astype(BF16)
    acc = y
    for c in range(D_FF // ff_chunk):
        sl = slice(c * ff_chunk, (c + 1) * ff_chunk)
        u = jnp.maximum(_dot(h, wu_ref[:, sl]), 0.0)
        acc = acc + _dot((u * u).astype(BF16), wd_ref[sl, :])
    o_ref[...] = _rms(acc, gf_ref[...])


def _mlp(y, attn, wo_bf16, g, wu_bf16, wd_bf16, g_final, bm):
    m = y.shape[0]
    row = lambda i: (i, 0)
    spec = pl.BlockSpec((bm, D_MODEL), row)
    return pl.pallas_call(
        functools.partial(_mlp_kernel, ff_chunk=1024),
        grid=(m // bm,),
        in_specs=[spec, spec, _const_spec((D_MODEL, D_MODEL)), _const_spec((1, D_MODEL)),
                  _const_spec((D_MODEL, D_FF)), _const_spec((D_FF, D_MODEL)),
                  _const_spec((1, D_MODEL))],
        out_specs=spec,
        out_shape=jax.ShapeDtypeStruct((m, D_MODEL), F32),
        compiler_params=_params(("parallel",)),
        name="mlp_final",
    )(y, attn, wo_bf16, g, wu_bf16, wd_bf16, g_final)


def _dattn_sample_kernel(pt_ref, q_ref, kn_ref, vn_ref, lq1, lk1, lq2, lk2, ck_hbm, cv_hbm,
                         o_ref, qs_ref, m_ref, l_ref, acc_ref, kbuf, vbuf, sems,
                         *, layer, n_total, lam_init):
    j = pl.program_id(1)
    g = pl.program_id(0) * pl.num_programs(1) + j
    pages_per_step = kbuf.shape[1]
    s_len = q_ref.shape[1]
    n_rows = 2 * DA_HEADS * s_len
    page_rows = kbuf.shape[2]
    lookahead = PAGE_SLOTS - 1

    def page_copies(step):
        slot = lax.rem(step, PAGE_SLOTS)
        out = []
        for p in range(pages_per_step):
            page = pt_ref[step * pages_per_step + p]
            out.append(pltpu.make_async_copy(ck_hbm.at[layer, page], kbuf.at[slot, p],
                                             sems.at[0, slot]))
            out.append(pltpu.make_async_copy(cv_hbm.at[layer, page], vbuf.at[slot, p],
                                             sems.at[1, slot]))
        return out

    @pl.when(g == 0)
    def _():
        for step in range(min(lookahead, n_total)):
            for cp in page_copies(step):
                cp.start()

    @pl.when(g + lookahead < n_total)
    def _():
        for cp in page_copies(g + lookahead):
            cp.start()

    for cp in page_copies(g):
        cp.wait()
    slot = lax.rem(g, PAGE_SLOTS)

    def same_head(r, c):
        return (c & (DA_HEADS - 1)) == (r >> 4)

    @pl.when(j == 0)
    def _():
        parts = []
        lane = lax.broadcasted_iota(jnp.int32, (s_len, LANES), 1)
        for hd in range(DA_HEADS):
            q = q_ref[hd].astype(F32)
            parts += [jnp.where(lane < DA_HEAD_DIM, q, 0.0), jnp.where(lane >= DA_HEAD_DIM, q, 0.0)]
        qs = jnp.concatenate(parts, axis=0).astype(BF16)
        qs_ref[...] = qs
        pad = jnp.zeros((LANES - DA_HEADS * s_len, LANES), F32)
        kn = jnp.concatenate([kn_ref[...], pad], axis=0).astype(BF16)
        vn = jnp.concatenate([vn_ref[...], pad], axis=0).astype(BF16)
        s = _dot_nt(qs, kn)
        r = lax.broadcasted_iota(jnp.int32, s.shape, 0)
        c = lax.broadcasted_iota(jnp.int32, s.shape, 1)
        causal = (c >> 2) <= (r & (s_len - 1))
        s = jnp.where(same_head(r, c) & causal, s, -jnp.inf)
        m = jnp.max(s, axis=-1, keepdims=True)
        p = jnp.exp2(s - m)
        m_ref[...] = m
        l_ref[...] = jnp.sum(p, axis=-1, keepdims=True)
        acc_ref[...] = _dot(p.astype(BF16), vn)

    def head_rows(buf, hd):
        rows = pl.ds(hd, page_rows // DA_HEADS, stride=DA_HEADS)
        return jnp.concatenate([buf[slot, p, rows, :].astype(BF16)
                                for p in range(pages_per_step)], axis=0)

    hr = n_rows // DA_HEADS
    s = jnp.concatenate([_dot_nt(qs_ref[hd * hr:(hd + 1) * hr, :], head_rows(kbuf, hd))
                         for hd in range(DA_HEADS)], axis=0)
    m_old = m_ref[...]
    m_new = jnp.maximum(m_old, jnp.max(s, axis=-1, keepdims=True))
    alpha = jnp.exp2(m_old - m_new)
    p = jnp.exp2(s - m_new)
    l_ref[...] = alpha * l_ref[...] + jnp.sum(p, axis=-1, keepdims=True)
    pv = jnp.concatenate([_dot(p[hd * hr:(hd + 1) * hr].astype(BF16), head_rows(vbuf, hd))
                          for hd in range(DA_HEADS)], axis=0)
    acc_ref[...] = alpha * acc_ref[...] + pv
    m_ref[...] = m_new

    @pl.when(j == pl.num_programs(1) - 1)
    def _():
        lam = _diff_lambda(lq1, lk1, lq2, lk2, lam_init)
        o = acc_ref[...] / l_ref[...]
        for h in range(DA_HEADS):
            r0 = 2 * h * s_len
            o_ref[:, h * LANES:(h + 1) * LANES] = (o[r0:r0 + s_len]
                                                   - lam * o[r0 + s_len:r0 + 2 * s_len])


def _dattn_sample(qh, kf, vf, cache_k, cache_v, layer, page_table, lams, lam_init, pages_per_step):
    b, n_pages = page_table.shape
    s = qh.shape[1] // b
    assert s == 8 and DA_HEADS == 4
    depth, n_pool, page = cache_k.shape[:3]
    page_rows = page * DA_HEADS
    ck = cache_k.reshape(depth, n_pool, page_rows, LANES)
    cv = cache_v.reshape(depth, n_pool, page_rows, LANES)
    pt = page_table.reshape(-1)
    steps = n_pages // pages_per_step
    qspec = pl.BlockSpec((DA_HEADS, s, LANES), lambda bi, j, pt_r: (0, bi, 0))
    nspec = pl.BlockSpec((s * DA_HEADS, LANES), lambda bi, j, pt_r: (bi, 0))
    lspec = pl.BlockSpec((1, DA_HEAD_DIM), lambda *_: (0, 0))

    hbm = pl.BlockSpec(memory_space=pl.ANY)
    ring = pltpu.VMEM((PAGE_SLOTS, pages_per_step, page_rows, LANES), F32)
    n_rows = 2 * DA_HEADS * s
    return pl.pallas_call(
        functools.partial(_dattn_sample_kernel, layer=layer, n_total=b * steps,
                          lam_init=lam_init),
        grid_spec=pltpu.PrefetchScalarGridSpec(
            num_scalar_prefetch=1,
            grid=(b, steps),
            in_specs=[qspec, nspec, nspec] + [lspec] * 4 + [hbm, hbm],
            out_specs=pl.BlockSpec((s, DA_HEADS * LANES), lambda bi, j, pt_r: (bi, 0)),
            scratch_shapes=[pltpu.VMEM((n_rows, LANES), BF16),
                            pltpu.VMEM((n_rows, 1), F32),
                            pltpu.VMEM((n_rows, 1), F32),
                            pltpu.VMEM((n_rows, LANES), F32),
                            ring, ring,
                            pltpu.SemaphoreType.DMA((2, PAGE_SLOTS))]),
        out_shape=jax.ShapeDtypeStruct((b * s, DA_HEADS * LANES), F32),
        compiler_params=_params(("arbitrary", "arbitrary")),
        name="dattn_sample",
    )(pt, qh, kf, vf, *lams, ck, cv)


def kernel(x_prompt, x_sample, cache_k, cache_v, state_ret, cache_mem_k, cache_mem_v, page_table, mem_prompt, g_mix, w_in, lambda_q1, lambda_k1, lambda_q2, lambda_k2, g_diff_sub, g_ret, w_out, g_cross, g_mem, w_cq, w_ck, w_cv, w_co, g_mlp, w_up, w_down, g_final):
    bp, t, d = x_prompt.shape
    bd, s, _ = x_sample.shape
    depth = w_in.shape[0]
    assert depth == 1, "one pass over the final norm per layer stack of depth 1"
    n_mem = mem_prompt.shape[1]
    past = page_table.shape[1] * cache_k.shape[2]
    l = 0
    lam_init = 0.8 - 0.6 * math.exp(-0.3 * l)

    bm_p = 512
    tabs_p = _rope_tables(np.arange(t), DA_HEAD_DIM) + _rope_tables(np.arange(t), RET_QK_DIM)
    bm_s = min(256, bd * s)
    pos_s = past + np.arange(s)
    tabs_s = tuple(np.tile(tb, (bm_s // s, 1)) for tb in
                   _rope_tables(pos_s, DA_HEAD_DIM) + _rope_tables(pos_s, RET_QK_DIM))

    row2 = lambda a: a.reshape(1, -1)
    bf = lambda a: a.astype(BF16)
    w_in_b, w_out_b = bf(w_in[l]), bf(w_out[l])
    w_cq_b, w_ck_b, w_cv_b, w_co_b = bf(w_cq[l]), bf(w_ck[l]), bf(w_cv[l]), bf(w_co[l])
    w_up_b, w_down_b = bf(w_up[l]), bf(w_down[l])
    lams = (row2(lambda_q1[l]), row2(lambda_k1[l]), row2(lambda_q2[l]), row2(lambda_k2[l]))
    g_mix_l, g_cross_l, g_mlp_l = row2(g_mix[l]), row2(g_cross[l]), row2(g_mlp[l])
    g_sub_l, g_ret_l, g_mem_l, g_fin = row2(g_diff_sub[l]), row2(g_ret[l]), row2(g_mem[l]), row2(g_final)

    xp = x_prompt.reshape(bp * t, d)
    kf, vf, qh, kh, vt, rq, rk, rv, rg = _in_proj(xp, g_mix_l, w_in_b, tabs_p, bm_p, t // bm_p)
    seq = lambda a: a.reshape(bp, t, GROUP_W)
    da = _dattn_prompt(qh, kh, vt, lams, lam_init, bp, t)
    ro, s_p = _ret_prompt(seq(rq), seq(rk), seq(rv))
    y1, cq = _merge(da, ro.reshape(bp * t, GROUP_W), rg, xp,
                    g_sub_l, g_ret_l, w_out_b, g_cross_l, w_cq_b, lam_init, bm_p)
    mk, mv = _mem_kv(mem_prompt.reshape(bp * n_mem, d), g_mem_l, w_ck_b, w_cv_b, 256)
    mk = mk.reshape(1, bp, n_mem, CA_HEADS, CA_HEAD_DIM)
    mv = mv.reshape(1, bp, n_mem, CA_HEADS, CA_HEAD_DIM)
    ca = _cross_prompt(cq.reshape(bp, t, d), mk, mv, 0, 512).reshape(bp * t, d)
    y_prompt = _mlp(y1, ca, w_co_b, g_mlp_l, w_up_b, w_down_b, g_fin, bm_p).reshape(bp, t, d)

    xs = x_sample.reshape(bd * s, d)
    kfs, vfs, qh, _, _, rq, rk, rv, rg = _in_proj(xs, g_mix_l, w_in_b, tabs_s, bm_s, 1)
    req = lambda a: a.reshape(bd, s, GROUP_W)
    da = _dattn_sample(qh, kfs, vfs, cache_k, cache_v, l, page_table, lams, lam_init, 16)
    ro, s_s = _ret_sample(req(rq), req(rk), req(rv), state_ret[l])
    y1, cq = _merge(da, ro.reshape(bd * s, GROUP_W), rg, xs,
                    g_sub_l, g_ret_l, w_out_b, g_cross_l, w_cq_b, lam_init, bm_s)
    ca = _cross_sample(cq, cache_mem_k, cache_mem_v, l, s)
    y_sample = _mlp(y1, ca, w_co_b, g_mlp_l, w_up_b, w_down_b, g_fin, bm_s).reshape(bd, s, d)

    return (y_prompt, y_sample,
            kf.reshape(1, bp, t, DA_HEADS, 2 * DA_HEAD_DIM),
            vf.reshape(1, bp, t, DA_HEADS, DA_V_DIM),
            s_p[None], mk, mv,
            kfs.reshape(1, bd, s, DA_HEADS, 2 * DA_HEAD_DIM),
            vfs.reshape(1, bd, s, DA_HEADS, DA_V_DIM),
            s_s[None])
```
